```python
import math
import jax
import jax.numpy as jnp
from jax import lax
import numpy as np

D_MODEL = 1024
BATCH = 32
SEQ = 2048
DEPTH = 2

MIX_WIDTH = D_MODEL
GROUP_WIDTH = MIX_WIDTH // 4
SSD_HEAD_DIM = 64
SSD_HEADS = GROUP_WIDTH // SSD_HEAD_DIM
SSD_N_GROUPS = 2
SSD_STATE = 128
SSD_CONV = 4
SSD_CHUNK = 128
SSD_XBC = GROUP_WIDTH + 2 * SSD_N_GROUPS * SSD_STATE
RET_HEAD_DIM = 64
RET_HEADS = GROUP_WIDTH // RET_HEAD_DIM
RET_CHUNK = 128
ROPE_BASE = 10000.0
LRU_BLOCKS = 4
LRU_BLOCK_DIM = GROUP_WIDTH // LRU_BLOCKS
LRU_CONV = 4
LRU_C = 8.0
GLA_HEADS = 4
GLA_VAL_DIM = GROUP_WIDTH // GLA_HEADS
GLA_KEY_DIM = GLA_VAL_DIM // 2
GLA_GATE_RANK = 16
GLA_GATE_NORM = 16.0
GLA_CHUNK = 64
D_FF = -(-8 * D_MODEL // (3 * 256)) * 256
DEEPNORM_ALPHA = (2 * DEPTH) ** 0.25
DEEPNORM_BETA = (8 * DEPTH) ** -0.25
EPS = 1e-5
IN_SPLITS = (GROUP_WIDTH, SSD_XBC, SSD_HEADS,
             GROUP_WIDTH, GROUP_WIDTH, GROUP_WIDTH, GROUP_WIDTH,
             GROUP_WIDTH, GROUP_WIDTH,
             GLA_HEADS * GLA_KEY_DIM, GLA_HEADS * GLA_KEY_DIM,
             GROUP_WIDTH, GLA_GATE_RANK, GROUP_WIDTH)
IN_WIDTH = sum(IN_SPLITS)

kernel_name = 'hybrid_ssd_retnet_rglru_gla_deepnorm_adaln'


def layer_norm(x, w, b):
    xf = x.astype(jnp.float32)
    mu = jnp.mean(xf, axis=-1, keepdims=True)
    var = jnp.mean(jnp.square(xf - mu), axis=-1, keepdims=True)
    return ((xf - mu) * lax.rsqrt(var + EPS) * w + b).astype(x.dtype)


def rms_norm(x, w):
    xf = x.astype(jnp.float32)
    return (xf * lax.rsqrt(jnp.mean(jnp.square(xf), axis=-1, keepdims=True) + EPS) * w).astype(x.dtype)


def head_group_norm(y, w):
    yf = y.astype(jnp.float32)
    mu = jnp.mean(yf, axis=-1, keepdims=True)
    var = jnp.mean(jnp.square(yf - mu), axis=-1, keepdims=True)
    return ((yf - mu) * lax.rsqrt(var + EPS) * w).astype(y.dtype)


def causal_depthwise_conv(x, w, b):
    width, ch = w.shape
    y = lax.conv_general_dilated(x, w[:, None, :], window_strides=(1,), padding=[(width - 1, 0)],
                                 dimension_numbers=('NWC', 'WIO', 'NWC'), feature_group_count=ch)
    return y + b


def apply_rotary(x, positions):
    half = x.shape[-1] // 2
    inv_freq = ROPE_BASE ** (-jnp.arange(half, dtype=jnp.float32) / half)
    ang = positions.astype(jnp.float32)[..., None] * inv_freq
    cos = jnp.cos(ang)[:, :, None, :]
    sin = jnp.sin(ang)[:, :, None, :]
    x1, x2 = x[..., :half], x[..., half:]
    return jnp.concatenate([x1 * cos - x2 * sin, x2 * cos + x1 * sin], axis=-1).astype(x.dtype)


def scan_chunk_states(states, chunk_decay):
    def step(h, inp):
        st, dec = inp
        return h * dec + st, h
    h0 = jnp.zeros_like(states[:, 0])
    _, prev = lax.scan(step, h0, (jnp.moveaxis(states, 1, 0), jnp.moveaxis(chunk_decay, 1, 0)))
    return jnp.moveaxis(prev, 0, 1)


def chunked_scalar_decay(q, k, v, log_a, chunk):
    bsz, seq, nh, n = q.shape
    p = v.shape[-1]
    nc = seq // chunk
    q = q.reshape(bsz, nc, chunk, nh, n)
    k = k.reshape(bsz, nc, chunk, nh, n)
    v = v.reshape(bsz, nc, chunk, nh, p)
    cum = jnp.cumsum(log_a.astype(jnp.float32).reshape(bsz, nc, chunk, nh), axis=2)
    causal = jnp.tril(jnp.ones((chunk, chunk), dtype=bool))
    seg = cum[:, :, :, None, :] - cum[:, :, None, :, :]
    decay = jnp.exp(jnp.where(causal[:, :, None], seg, -jnp.inf))
    scores = jnp.einsum('bcthn,bcshn->bctsh', q, k) * decay
    y_intra = jnp.einsum('bctsh,bcshp->bcthp', scores, v)
    to_end = jnp.exp(cum[:, :, -1:, :] - cum)
    states = jnp.einsum('bcsh,bcshn,bcshp->bchnp', to_end, k, v).astype(jnp.float32)
    chunk_decay = jnp.exp(cum[:, :, -1, :])[..., None, None]
    prev = scan_chunk_states(states, chunk_decay)
    y_inter = jnp.einsum('bcth,bcthn,bchnp->bcthp', jnp.exp(cum), q, prev)
    return (y_intra + y_inter).reshape(bsz, seq, nh, p).astype(v.dtype)


def chunked_gla(q, k, v, log_a, chunk):
    bsz, seq, nh, kd = q.shape
    vd = v.shape[-1]
    nc = seq // chunk
    q = q.reshape(bsz, nc, chunk, nh, kd)
    k = k.reshape(bsz, nc, chunk, nh, kd)
    v = v.reshape(bsz, nc, chunk, nh, vd)
    cum = jnp.cumsum(log_a.astype(jnp.float32).reshape(bsz, nc, chunk, nh, kd), axis=2)
    q_in = q * jnp.exp(cum)
    k_in = k * jnp.exp(-cum)
    k_end = k * jnp.exp(cum[:, :, -1:] - cum)
    causal = jnp.tril(jnp.ones((chunk, chunk), dtype=bool))
    scores = jnp.where(causal, jnp.einsum('bcthk,bcshk->bchts', q_in, k_in), 0.0)
    y_intra = jnp.einsum('bchts,bcshv->bcthv', scores, v)
    states = jnp.einsum('bcshk,bcshv->bchkv', k_end, v).astype(jnp.float32)
    chunk_decay = jnp.exp(cum[:, :, -1])[..., None]
    prev = scan_chunk_states(states, chunk_decay)
    y_inter = jnp.einsum('bcthk,bchkv->bcthv', q_in, prev)
    return (y_intra + y_inter).reshape(bsz, seq, nh, vd).astype(v.dtype)


def linear_recurrence(a, u):
    def combine(left, right):
        a_l, u_l = left
        a_r, u_r = right
        return a_l * a_r, a_r * u_l + u_r
    _, h = lax.associative_scan(combine, (a, u), axis=1)
    return h


def ssd_group(z, xbc, dt_raw, conv_w, conv_b, dt_bias, a_log, d_skip, norm_w):
    bsz, seq, _ = z.shape
    xbc = jax.nn.silu(causal_depthwise_conv(xbc, conv_w, conv_b))
    xs, b_in, c_in = jnp.split(xbc, [GROUP_WIDTH, GROUP_WIDTH + SSD_N_GROUPS * SSD_STATE], axis=-1)
    xs = xs.reshape(bsz, seq, SSD_HEADS, SSD_HEAD_DIM)
    rep = SSD_HEADS // SSD_N_GROUPS
    b_in = jnp.repeat(b_in.reshape(bsz, seq, SSD_N_GROUPS, SSD_STATE), rep, axis=2)
    c_in = jnp.repeat(c_in.reshape(bsz, seq, SSD_N_GROUPS, SSD_STATE), rep, axis=2)
    dt = jax.nn.softplus((dt_raw + dt_bias).astype(jnp.float32))
    log_a = -jnp.exp(a_log.astype(jnp.float32)) * dt
    y = chunked_scalar_decay(c_in, b_in, xs * dt[..., None], log_a, SSD_CHUNK)
    y = (y + xs * d_skip[:, None]).reshape(bsz, seq, GROUP_WIDTH)
    return rms_norm(y * jax.nn.silu(z), norm_w)


def retention_group(q, k, v, g, positions, norm_w):
    bsz, seq, _ = q.shape
    shp = (bsz, seq, RET_HEADS, RET_HEAD_DIM)
    q = apply_rotary(q.reshape(shp), positions)
    k = apply_rotary(k.reshape(shp), positions) * (RET_HEAD_DIM ** -0.5)
    log_gamma = jnp.log1p(-jnp.exp2(-5.0 - jnp.arange(RET_HEADS, dtype=jnp.float32)))
    log_a = jnp.broadcast_to(log_gamma, (bsz, seq, RET_HEADS))
    y = chunked_scalar_decay(q, k, v.reshape(shp), log_a, RET_CHUNK)
    y = head_group_norm(y, norm_w.reshape(RET_HEADS, RET_HEAD_DIM)).reshape(bsz, seq, GROUP_WIDTH)
    return y * jax.nn.silu(g)


def rglru_group(gate, xr, conv_w, conv_b, wa, ba, wx, bx, lam):
    bsz, seq, _ = xr.shape
    xr = causal_depthwise_conv(xr, conv_w, conv_b)
    xb = xr.reshape(bsz, seq, LRU_BLOCKS, LRU_BLOCK_DIM)
    r = jax.nn.sigmoid(jnp.einsum('bski,kij->bskj', xb, wa).reshape(bsz, seq, GROUP_WIDTH) + ba)
    i = jax.nn.sigmoid(jnp.einsum('bski,kij->bskj', xb, wx).reshape(bsz, seq, GROUP_WIDTH) + bx)
    log_a = -LRU_C * r.astype(jnp.float32) * jax.nn.softplus(-lam.astype(jnp.float32))
    a = jnp.exp(log_a)
    u = jnp.sqrt(-jnp.expm1(2.0 * log_a)) * (i * xr).astype(jnp.float32)
    h = linear_recurrence(a, u).astype(xr.dtype)
    return h * jax.nn.gelu(gate)


def gla_group(q, k, v, g_low, r, wg2, bg, norm_w):
    bsz, seq, _ = q.shape
    q = q.reshape(bsz, seq, GLA_HEADS, GLA_KEY_DIM) * (GLA_KEY_DIM ** -0.5)
    k = k.reshape(bsz, seq, GLA_HEADS, GLA_KEY_DIM)
    v = v.reshape(bsz, seq, GLA_HEADS, GLA_VAL_DIM)
    log_a = jax.nn.log_sigmoid((g_low @ wg2 + bg).astype(jnp.float32)) / GLA_GATE_NORM
    log_a = log_a.reshape(bsz, seq, GLA_HEADS, GLA_KEY_DIM)
    o = chunked_gla(q, k, v, log_a, GLA_CHUNK)
    o = rms_norm(o, norm_w.reshape(GLA_HEADS, GLA_VAL_DIM)).reshape(bsz, seq, GROUP_WIDTH)
    return o * jax.nn.silu(r)


def hybrid_mixer(h, positions, w_in, ssd_conv_w, ssd_conv_b, ssd_dt_bias, ssd_a_log, ssd_d, ssd_norm_w,
                 ret_norm_w, lru_conv_w, lru_conv_b, lru_wa, lru_ba, lru_wx, lru_bx, lru_lambda,
                 gla_wg2, gla_bg, gla_norm_w, w_out):
    proj = jnp.einsum('bsd,de->bse', h, w_in)
    split_at = np.cumsum(IN_SPLITS)[:-1].tolist()
    (ssd_z, ssd_xbc, ssd_dt, ret_q, ret_k, ret_v, ret_g, lru_gate, lru_x,
     gla_q, gla_k, gla_v, gla_glow, gla_r) = jnp.split(proj, split_at, axis=-1)
    y_ssd = ssd_group(ssd_z, ssd_xbc, ssd_dt, ssd_conv_w, ssd_conv_b, ssd_dt_bias, ssd_a_log, ssd_d, ssd_norm_w)
    y_ret = retention_group(ret_q, ret_k, ret_v, ret_g, positions, ret_norm_w)
    y_lru = rglru_group(lru_gate, lru_x, lru_conv_w, lru_conv_b, lru_wa, lru_ba, lru_wx, lru_bx, lru_lambda)
    y_gla = gla_group(gla_q, gla_k, gla_v, gla_glow, gla_r, gla_wg2, gla_bg, gla_norm_w)
    y = jnp.concatenate([y_ssd, y_ret, y_lru, y_gla], axis=-1)
    return jnp.einsum('bse,ed->bsd', y, w_out)


def swiglu(h, w_up, w_down):
    g, u = jnp.split(jnp.einsum('bsd,df->bsf', h, w_up), 2, axis=-1)
    return jnp.einsum('bsf,fd->bsd', jax.nn.silu(g) * u, w_down)


def setup_inputs(seed: int = 0) -> dict:
    key = jax.random.key(seed)
    ks = iter(jax.random.split(key, 40))
    f32 = jnp.float32
    L, D = DEPTH, D_MODEL

    def nrm(shape, scale):
        return jax.random.normal(next(ks), shape, f32) * scale

    x = nrm((BATCH, SEQ, D), 1.0)
    c = nrm((BATCH, D), 1.0)
    offsets = jax.random.randint(next(ks), (BATCH, 1), 0, 4096, dtype=jnp.int32)
    positions = (offsets + jnp.arange(SEQ, dtype=jnp.int32)[None, :]).astype(jnp.int32)
    w_ada = nrm((L, D, 6 * D), D ** -0.5)
    b_ada = nrm((L, 6 * D), 0.02)
    w_in = nrm((L, D, IN_WIDTH), D ** -0.5)
    ssd_conv_w = nrm((L, SSD_CONV, SSD_XBC), SSD_CONV ** -0.5)
    ssd_conv_b = nrm((L, SSD_XBC), 0.02)
    dt0 = jnp.exp(jax.random.uniform(next(ks), (L, SSD_HEADS), f32, math.log(1e-3), math.log(1e-1)))
    ssd_dt_bias = dt0 + jnp.log(-jnp.expm1(-dt0))
    ssd_a_log = jnp.log(jax.random.uniform(next(ks), (L, SSD_HEADS), f32, 1.0, 16.0))
    ssd_d = 1.0 + nrm((L, SSD_HEADS), 0.02)
    ssd_norm_w = 1.0 + nrm((L, GROUP_WIDTH), 0.02)
    ret_norm_w = 1.0 + nrm((L, GROUP_WIDTH), 0.02)
    lru_conv_w = nrm((L, LRU_CONV, GROUP_WIDTH), LRU_CONV ** -0.5)
    lru_conv_b = nrm((L, GROUP_WIDTH), 0.02)
    lru_wa = nrm((L, LRU_BLOCKS, LRU_BLOCK_DIM, LRU_BLOCK_DIM), LRU_BLOCK_DIM ** -0.5)
    lru_ba = nrm((L, GROUP_WIDTH), 0.02)
    lru_wx = nrm((L, LRU_BLOCKS, LRU_BLOCK_DIM, LRU_BLOCK_DIM), LRU_BLOCK_DIM ** -0.5)
    lru_bx = nrm((L, GROUP_WIDTH), 0.02)
    a0 = jax.random.uniform(next(ks), (L, GROUP_WIDTH), f32, 0.9, 0.999)
    s0 = a0 ** (1.0 / LRU_C)
    lru_lambda = jnp.log(s0) - jnp.log1p(-s0)
    gla_wg2 = nrm((L, GLA_GATE_RANK, GLA_HEADS * GLA_KEY_DIM), GLA_GATE_RANK ** -0.5)
    gla_bg = nrm((L, GLA_HEADS * GLA_KEY_DIM), 0.02)
    gla_norm_w = 1.0 + nrm((L, GROUP_WIDTH), 0.02)
    w_out = nrm((L, MIX_WIDTH, D), MIX_WIDTH ** -0.5 * DEEPNORM_BETA)
    ln1_w = 1.0 + nrm((L, D), 0.02)
    ln1_b = nrm((L, D), 0.02)
    ffn_w_up = nrm((L, D, 2 * D_FF), D ** -0.5)
    ffn_w_down = nrm((L, D_FF, D), D_FF ** -0.5 * DEEPNORM_BETA)
    ln2_w = 1.0 + nrm((L, D), 0.02)
    ln2_b = nrm((L, D), 0.02)
    return {'x': x, 'c': c, 'positions': positions, 'w_ada': w_ada, 'b_ada': b_ada, 'w_in': w_in,
            'ssd_conv_w': ssd_conv_w, 'ssd_conv_b': ssd_conv_b, 'ssd_dt_bias': ssd_dt_bias,
            'ssd_a_log': ssd_a_log, 'ssd_d': ssd_d, 'ssd_norm_w': ssd_norm_w, 'ret_norm_w': ret_norm_w,
            'lru_conv_w': lru_conv_w, 'lru_conv_b': lru_conv_b, 'lru_wa': lru_wa, 'lru_ba': lru_ba,
            'lru_wx': lru_wx, 'lru_bx': lru_bx, 'lru_lambda': lru_lambda, 'gla_wg2': gla_wg2,
            'gla_bg': gla_bg, 'gla_norm_w': gla_norm_w, 'w_out': w_out, 'ln1_w': ln1_w, 'ln1_b': ln1_b,
            'ffn_w_up': ffn_w_up, 'ffn_w_down': ffn_w_down, 'ln2_w': ln2_w, 'ln2_b': ln2_b}


def reference(x, c, positions, w_ada, b_ada, w_in, ssd_conv_w, ssd_conv_b, ssd_dt_bias, ssd_a_log, ssd_d,
              ssd_norm_w, ret_norm_w, lru_conv_w, lru_conv_b, lru_wa, lru_ba, lru_wx, lru_bx, lru_lambda,
              gla_wg2, gla_bg, gla_norm_w, w_out, ln1_w, ln1_b, ffn_w_up, ffn_w_down, ln2_w, ln2_b):
    c_act = jax.nn.silu(c)
    for l in range(DEPTH):
        mod = c_act @ w_ada[l] + b_ada[l]
        sh_m, sc_m, g_m, sh_f, sc_f, g_f = [m[:, None, :] for m in jnp.split(mod, 6, axis=-1)]
        h = x * (1.0 + sc_m) + sh_m
        y = hybrid_mixer(h, positions, w_in[l], ssd_conv_w[l], ssd_conv_b[l], ssd_dt_bias[l], ssd_a_log[l],
                         ssd_d[l], ssd_norm_w[l], ret_norm_w[l], lru_conv_w[l], lru_conv_b[l], lru_wa[l],
                         lru_ba[l], lru_wx[l], lru_bx[l], lru_lambda[l], gla_wg2[l], gla_bg[l],
                         gla_norm_w[l], w_out[l])
        x = layer_norm(DEEPNORM_ALPHA * x + g_m * y, ln1_w[l], ln1_b[l])
        h = x * (1.0 + sc_f) + sh_f
        y = swiglu(h, ffn_w_up[l], ffn_w_down[l])
        x = layer_norm(DEEPNORM_ALPHA * x + g_f * y, ln2_w[l], ln2_b[l])
    return x
```

```python
import functools
import math

import numpy as np
import jax
import jax.numpy as jnp
from jax import lax
from jax.experimental import pallas as pl
from jax.experimental.pallas import tpu as pltpu

F32 = jnp.float32
BF16 = jnp.bfloat16

D_MODEL = 1024
GROUP_WIDTH = 256
HEAD_DIM = 64
N_HEADS = 4
SSD_STATE = 128
SSD_XBC = 768
CONV_K = 4
CHUNK = 128
GLA_CHUNK = 64
GLA_KEY_DIM = 32
GLA_GATE_RANK = 16
GLA_GATE_NORM = 16.0
ROPE_BASE = 10000.0
LRU_C = 8.0
D_FF = 2816
DEPTH = 2
DEEPNORM_ALPHA = (2 * DEPTH) ** 0.25
EPS = 1e-5
IN_SPLITS = (256, 768, 4, 256, 256, 256, 256, 256, 256, 128, 128, 256, 16, 256)

LANES = 128
SUBLANES = 8

OFF_Z = 0
OFF_XBC = 256
OFF_DT = 1024
OFF_RQ = 1280
OFF_RK = 1536
OFF_RV = 1792
OFF_RG = 2048
OFF_LG = 2304
OFF_LX = 2560
OFF_GQ = 2816
OFF_GK = 2944
OFF_GV = 3072
OFF_GR = 3328
OFF_GLOW = 3584
PROJ_W = 3712

MIX_TOKENS = 256
FFN_TOKENS = 512
FFN_BLOCK = 256
VMEM_LIMIT = 56 * 1024 * 1024


def _dot(a, b):
    return jnp.dot(a, b, preferred_element_type=F32)


def _dot_nt(a, b):
    return lax.dot_general(a, b, (((1,), (1,)), ((), ())), preferred_element_type=F32)


def _sigmoid(x):
    return 1.0 / (1.0 + jnp.exp(-x))


def _silu(x):
    return x * _sigmoid(x)


def _softplus(x):
    return jnp.maximum(x, 0.0) + jnp.log1p(jnp.exp(-jnp.abs(x)))


def _gelu_tanh(x):
    return 0.5 * x * (1.0 + jnp.tanh(math.sqrt(2.0 / math.pi) * (x + 0.044715 * (x * x * x))))


def _layer_norm(v, w, b):
    mu = jnp.mean(v, axis=-1, keepdims=True)
    vc = v - mu
    var = jnp.mean(vc * vc, axis=-1, keepdims=True)
    return vc * lax.rsqrt(var + EPS) * w + b


def _split3_dot(m_bf16, v):
    hi = v.astype(BF16)
    r1 = v - hi.astype(F32)
    mid = r1.astype(BF16)
    lo = (r1 - mid.astype(F32)).astype(BF16)
    return _dot(m_bf16, hi) + _dot(m_bf16, mid) + _dot(m_bf16, lo)


def _iota(shape, axis):
    return lax.broadcasted_iota(jnp.int32, shape, axis)


def _block_of(shape, axis, size):
    return _iota(shape, axis) >> (size.bit_length() - 1)


def _within(shape, axis, size):
    return _iota(shape, axis) & (size - 1)


def _ada_kernel(c_ref, w_ref, b_ref, o_ref):
    c_act = _silu(c_ref[...])
    o_ref[0] = jnp.dot(c_act, w_ref[0], preferred_element_type=F32,
                       precision=lax.Precision.HIGHEST) + b_ref[0]


def _ada_call(c, w_ada, b_ada):
    depth, d, n = w_ada.shape
    bsz = c.shape[0]
    nb = n // d
    return pl.pallas_call(
        _ada_kernel,
        grid=(depth, nb),
        in_specs=[pl.BlockSpec((bsz, d), lambda l, j: (0, 0)),
                  pl.BlockSpec((1, d, d), lambda l, j: (l, 0, j)),
                  pl.BlockSpec((1, 1, d), lambda l, j: (l, 0, j))],
        out_specs=pl.BlockSpec((1, bsz, d), lambda l, j: (l, 0, j)),
        out_shape=jax.ShapeDtypeStruct((depth, bsz, n), F32),
        compiler_params=pltpu.CompilerParams(
            dimension_semantics=("arbitrary", "arbitrary"), vmem_limit_bytes=VMEM_LIMIT),
        name="adaln_mod",
    )(c, w_ada, b_ada.reshape(depth, 1, n))


def _rope_kernel(pos_ref, freq_ref, sign_ref, cos_ref, sin_ref):
    n_rows = pos_ref.shape[1]

    def body(r, carry):
        row = pos_ref[0, pl.ds(r, 1), :].astype(F32)
        col = jnp.broadcast_to(row, (LANES, LANES)).T
        ang = col * freq_ref[...]
        start = pl.multiple_of(r * LANES, LANES)
        cos_ref[0, pl.ds(start, LANES), :] = jnp.cos(ang)
        sin_ref[0, pl.ds(start, LANES), :] = jnp.sin(ang) * sign_ref[...]
        return carry

    lax.fori_loop(0, n_rows, body, 0)


def _rope_call(positions):
    bsz, seq = positions.shape
    half = HEAD_DIM // 2
    inv_freq = ROPE_BASE ** (-jnp.arange(half, dtype=F32) / half)
    lane = np.arange(LANES)
    freq = inv_freq[lane % half].reshape(1, LANES)
    sign = jnp.asarray(np.where((lane % HEAD_DIM) < half, -1.0, 1.0), F32).reshape(1, LANES)
    pos3 = positions.reshape(bsz, seq // LANES, LANES)
    out = jax.ShapeDtypeStruct((bsz, seq, LANES), F32)
    return pl.pallas_call(
        _rope_kernel,
        grid=(bsz,),
        in_specs=[pl.BlockSpec((1, seq // LANES, LANES), lambda b: (b, 0, 0)),
                  pl.BlockSpec((1, LANES), lambda b: (0, 0)),
                  pl.BlockSpec((1, LANES), lambda b: (0, 0))],
        out_specs=[pl.BlockSpec((1, seq, LANES), lambda b: (b, 0, 0)),
                   pl.BlockSpec((1, seq, LANES), lambda b: (b, 0, 0))],
        out_shape=[out, out],
        compiler_params=pltpu.CompilerParams(
            dimension_semantics=("arbitrary",), vmem_limit_bytes=VMEM_LIMIT),
        name="rope_table",
    )(pos3, freq, sign)


def _swap_halves(v):
    width = v.shape[1]
    fwd = pltpu.roll(v, HEAD_DIM // 2, axis=1)
    bwd = pltpu.roll(v, width - HEAD_DIM // 2, axis=1)
    first_half = _within(v.shape, 1, HEAD_DIM) < HEAD_DIM // 2
    return jnp.where(first_half, bwd, fwd)


def _causal_conv(ext_ref, n_rows, w_ref, b_ref):
    acc = b_ref[...] + w_ref[0:1, :] * ext_ref[pl.ds(SUBLANES - CONV_K + 1, n_rows), :]
    for k in range(1, CONV_K):
        acc = acc + w_ref[k:k + 1, :] * ext_ref[pl.ds(SUBLANES - CONV_K + 1 + k, n_rows), :]
    return acc


def _mixer_kernel(x_ref, mod_ref, cos_ref, sin_ref, w_in_ref, w_out_ref,
                  ssd_cw_ref, ssd_cb_ref, dtb_ref, alog_ref, dskip_ref, ssd_nw_ref, ret_nw_ref,
                  lru_cw_ref, lru_cb_ref, lru_w_ref, lru_b_ref, lam_ref,
                  wg2_ref, bg_ref, gla_nw_ref, ln_w_ref, ln_b_ref,
                  o_ref,
                  proj, ybuf, xbc_ext, lx_ext, ssd_st, ret_st, gla_st, lru_st,
                  ret_l, ret_in, ret_end, ret_dec):
    tc = x_ref.shape[1]
    n_chunks = tc // CHUNK
    step = pl.program_id(1)

    lane128 = _iota((CHUNK, LANES), 1)
    row128 = _iota((CHUNK, LANES), 0)
    low_half = lane128 < HEAD_DIM
    causal = row128 >= lane128
    tril = jnp.where(causal, 1.0, 0.0).astype(BF16)
    sq = (CHUNK, LANES)
    same_gla_chunk = _block_of(sq, 0, GLA_CHUNK) == _block_of(sq, 1, GLA_CHUNK)
    tril_gla = jnp.where(causal, jnp.where(same_gla_chunk, 1.0, 0.0), 0.0).astype(BF16)
    pair_diag = _block_of(sq, 0, HEAD_DIM) == _block_of(sq, 1, HEAD_DIM)

    head_of_lane = _block_of((1, GROUP_WIDTH), 1, HEAD_DIM)
    log_gamma = jnp.zeros((1, GROUP_WIDTH), F32)
    for h in range(N_HEADS):
        log_gamma = jnp.where(head_of_lane == h, math.log1p(-(2.0 ** (-5 - h))), log_gamma)

    @pl.when(step == 0)
    def _init():
        ssd_st[...] = jnp.zeros_like(ssd_st)
        ret_st[...] = jnp.zeros_like(ret_st)
        gla_st[...] = jnp.zeros_like(gla_st)
        lru_st[...] = jnp.zeros_like(lru_st)
        xbc_ext[0:SUBLANES, :] = jnp.zeros((SUBLANES, SSD_XBC), F32)
        lx_ext[0:SUBLANES, :] = jnp.zeros((SUBLANES, GROUP_WIDTH), F32)
        t_col = _iota((CHUNK, GROUP_WIDTH), 0).astype(F32)
        ret_in[...] = jnp.exp((t_col + 1.0) * log_gamma)
        ret_end[...] = jnp.exp((CHUNK - 1.0 - t_col) * log_gamma)
        ret_dec[...] = jnp.exp(float(CHUNK) * jnp.broadcast_to(log_gamma, (SUBLANES, GROUP_WIDTH)))
        dist = (row128 - lane128).astype(F32)
        for h in range(N_HEADS):
            lg = math.log1p(-(2.0 ** (-5 - h)))
            ret_l[h // 2, :, (h % 2) * LANES:(h % 2 + 1) * LANES] = jnp.where(
                causal, jnp.exp(dist * lg), 0.0)

    x = x_ref[0]
    shift = mod_ref[0, 0:1, :]
    scale = mod_ref[0, 1:2, :]
    gate = mod_ref[0, 2:3, :]
    h_in = (x * (1.0 + scale) + shift).astype(BF16)
    proj[...] = _dot(h_in, w_in_ref[...])

    xbc_ext[pl.ds(SUBLANES, tc), :] = proj[:, OFF_XBC:OFF_XBC + SSD_XBC]
    proj[:, OFF_XBC:OFF_XBC + SSD_XBC] = _silu(_causal_conv(xbc_ext, tc, ssd_cw_ref, ssd_cb_ref))
    xbc_ext[0:SUBLANES, :] = xbc_ext[pl.ds(tc, SUBLANES), :]

    lx_ext[pl.ds(SUBLANES, tc), :] = proj[:, OFF_LX:OFF_LX + GROUP_WIDTH]
    proj[:, OFF_LX:OFF_LX + GROUP_WIDTH] = _causal_conv(lx_ext, tc, lru_cw_ref, lru_cb_ref)
    lx_ext[0:SUBLANES, :] = lx_ext[pl.ds(tc, SUBLANES), :]

    gsq = (GROUP_WIDTH, GROUP_WIDTH)
    head_avg = jnp.where(_block_of(gsq, 0, HEAD_DIM) == _block_of(gsq, 1, HEAD_DIM),
                         1.0 / HEAD_DIM, 0.0).astype(BF16)
    gla_k_diag = _block_of((GROUP_WIDTH, LANES), 0, GLA_CHUNK) == _block_of((GROUP_WIDTH, LANES), 1, GLA_KEY_DIM)
    gla_v_diag = _block_of(gsq, 0, GLA_CHUNK) == _block_of(gsq, 1, HEAD_DIM)
    gla_causal = _iota((GLA_CHUNK, GROUP_WIDTH), 0) >= _within((GLA_CHUNK, GROUP_WIDTH), 1, GLA_CHUNK)
    gla_st_diag = (_block_of((LANES, GROUP_WIDTH), 0, GLA_KEY_DIM)
                   == _block_of((LANES, GROUP_WIDTH), 1, HEAD_DIM))
    gla_row_chunk = _block_of((CHUNK, GROUP_WIDTH), 0, GLA_CHUNK)

    for c in range(n_chunks):
        rows = pl.ds(c * CHUNK, CHUNK)

        dt = _softplus(proj[rows, OFF_DT:OFF_DT + GROUP_WIDTH] + dtb_ref[...])
        log_a = -jnp.exp(alog_ref[...]) * dt
        cum = _split3_dot(tril, log_a)
        cum_last = cum[CHUNK - 1:CHUNK, :]
        to_end = jnp.exp(cum_last - cum)
        from_start = jnp.exp(cum)
        xs = proj[rows, OFF_XBC:OFF_XBC + GROUP_WIDTH]
        xdt = xs * dt
        y_groups = []
        for g in range(2):
            gl = slice(g * LANES, (g + 1) * LANES)
            b_g = proj[rows, OFF_XBC + GROUP_WIDTH + g * LANES:OFF_XBC + GROUP_WIDTH + (g + 1) * LANES]
            c_g = proj[rows, OFF_XBC + 2 * GROUP_WIDTH + g * LANES:
                       OFF_XBC + 2 * GROUP_WIDTH + (g + 1) * LANES].astype(BF16)
            b_t = b_g.T.astype(BF16)
            scores = _dot(c_g, b_t)
            cum_g = cum[:, gl]
            cum_sw = pltpu.roll(cum_g, HEAD_DIM, axis=1)
            col_a = jnp.where(low_half, cum_g, cum_sw)
            col_b = jnp.where(low_half, cum_sw, cum_g)
            cum_t = cum_g.T
            l_a = jnp.exp(jnp.where(causal, col_a - cum_t[0:1, :], -jnp.inf))
            l_b = jnp.exp(jnp.where(causal, col_b - cum_t[HEAD_DIM:HEAD_DIM + 1, :], -jnp.inf))
            p = jnp.concatenate([scores * l_a, scores * l_b], axis=1).astype(BF16)
            xdt_g = xdt[:, gl]
            v_bd = jnp.concatenate([jnp.where(low_half, xdt_g, 0.0),
                                    jnp.where(low_half, 0.0, xdt_g)], axis=0).astype(BF16)
            y_intra = _dot(p, v_bd)
            prev = ssd_st[g]
            y_inter = from_start[:, gl] * _dot(c_g, prev.astype(BF16))
            new_states = _dot(b_t, (xdt_g * to_end[:, gl]).astype(BF16))
            ssd_st[g] = prev * jnp.exp(cum_last[:, gl]) + new_states
            y_groups.append(y_intra + y_inter)
        y = jnp.concatenate(y_groups, axis=1) + xs * dskip_ref[...]
        y = y * _silu(proj[rows, OFF_Z:OFF_Z + GROUP_WIDTH])
        y = y * lax.rsqrt(jnp.mean(y * y, axis=-1, keepdims=True) + EPS) * ssd_nw_ref[...]
        ybuf[rows, 0:GROUP_WIDTH] = y.astype(BF16)

        cos = cos_ref[0, rows, :]
        sin = sin_ref[0, rows, :]
        cos2 = jnp.concatenate([cos, cos], axis=1)
        sin2 = jnp.concatenate([sin, sin], axis=1)
        q = proj[rows, OFF_RQ:OFF_RQ + GROUP_WIDTH]
        k = proj[rows, OFF_RK:OFF_RK + GROUP_WIDTH]
        q = q * cos2 + _swap_halves(q) * sin2
        k = (k * cos2 + _swap_halves(k) * sin2) * (HEAD_DIM ** -0.5)
        v = proj[rows, OFF_RV:OFF_RV + GROUP_WIDTH]
        v_end = v * ret_end[...]
        q_bf = q.astype(BF16)
        y_pairs = []
        for p_i in range(2):
            pl_ = slice(p_i * LANES, (p_i + 1) * LANES)
            q_p = q_bf[:, pl_]
            k_t = k[:, pl_].T
            k_t_bf = k_t.astype(BF16)
            k_bd = jnp.concatenate([jnp.where(row128 < HEAD_DIM, k_t, 0.0),
                                    jnp.where(row128 < HEAD_DIM, 0.0, k_t)], axis=1).astype(BF16)
            scores = _dot(q_p, k_bd)
            p = (scores * ret_l[p_i]).astype(BF16)
            v_p = v[:, pl_]
            v_bd = jnp.concatenate([jnp.where(low_half, v_p, 0.0),
                                    jnp.where(low_half, 0.0, v_p)], axis=0).astype(BF16)
            y_intra = _dot(p, v_bd)
            prev = ret_st[p_i]
            y_inter = ret_in[:, pl_] * _dot(q_p, prev.astype(BF16))
            new_states = jnp.where(pair_diag, _dot(k_t_bf, v_end[:, pl_].astype(BF16)), 0.0)
            ret_st[p_i] = prev * ret_dec[0:1, pl_] + new_states
            y_pairs.append(y_intra + y_inter)
        y = jnp.concatenate(y_pairs, axis=1)
        mu = _dot(y.astype(BF16), head_avg)
        yc = y - mu
        var = _dot((yc * yc).astype(BF16), head_avg)
        y = yc * lax.rsqrt(var + EPS) * ret_nw_ref[...]
        y = y * _silu(proj[rows, OFF_RG:OFF_RG + GROUP_WIDTH])
        ybuf[rows, GROUP_WIDTH:2 * GROUP_WIDTH] = y.astype(BF16)

        g_low = proj[rows, OFF_GLOW:OFF_GLOW + LANES].astype(BF16)
        log_a = -_softplus(-(_dot(g_low, wg2_ref[...]) + bg_ref[...])) * (1.0 / GLA_GATE_NORM)
        cum = _split3_dot(tril_gla, log_a)
        q = proj[rows, OFF_GQ:OFF_GQ + LANES] * (GLA_KEY_DIM ** -0.5)
        k = proj[rows, OFF_GK:OFF_GK + LANES]
        v = proj[rows, OFF_GV:OFF_GV + GROUP_WIDTH]
        first = row128 < GLA_CHUNK
        cum_end = jnp.where(first, cum[GLA_CHUNK - 1:GLA_CHUNK, :], cum[CHUNK - 1:CHUNK, :])
        q_in = (q * jnp.exp(cum)).astype(BF16)
        k_in = k * jnp.exp(-cum)
        k_end_t = (k * jnp.exp(cum_end - cum)).T.astype(BF16)
        dec_t = jnp.exp(cum_end).T
        dec_sw = pltpu.roll(dec_t, GLA_CHUNK, axis=1)
        dec_by_chunk = (jnp.where(low_half, dec_t, dec_sw), jnp.where(low_half, dec_sw, dec_t))
        y_chunks = []
        for cc in range(CHUNK // GLA_CHUNK):
            rr = slice(cc * GLA_CHUNK, (cc + 1) * GLA_CHUNK)
            k_rep = jnp.concatenate([k_in[rr, :]] * N_HEADS, axis=0)
            k_bd = jnp.where(gla_k_diag, k_rep, 0.0).astype(BF16)
            scores = _dot_nt(q_in[rr, :], k_bd)
            scores = jnp.where(gla_causal, scores, 0.0).astype(BF16)
            v_rep = jnp.concatenate([v[rr, :]] * N_HEADS, axis=0)
            v_bd = jnp.where(gla_v_diag, v_rep, 0.0).astype(BF16)
            y_intra = _dot(scores, v_bd)
            prev = gla_st[...]
            y_inter = _dot(q_in[rr, :], prev.astype(BF16))
            v_only = jnp.where(gla_row_chunk == cc, v, 0.0).astype(BF16)
            new_states = jnp.where(gla_st_diag, _dot(k_end_t, v_only), 0.0)
            dec = jnp.concatenate([dec_by_chunk[cc]] * 2, axis=1)
            gla_st[...] = prev * dec + new_states
            y_chunks.append(y_intra + y_inter)
        y = jnp.concatenate(y_chunks, axis=0)
        ms = _dot((y * y).astype(BF16), head_avg)
        y = y * lax.rsqrt(ms + EPS) * gla_nw_ref[...]
        y = y * _silu(proj[rows, OFF_GR:OFF_GR + GROUP_WIDTH])
        ybuf[rows, 3 * GROUP_WIDTH:4 * GROUP_WIDTH] = y.astype(BF16)

    xr = proj[:, OFF_LX:OFF_LX + GROUP_WIDTH]
    gates = _dot(xr.astype(BF16), lru_w_ref[...]) + lru_b_ref[...]
    r_gate = _sigmoid(gates[:, 0:GROUP_WIDTH])
    i_gate = _sigmoid(gates[:, GROUP_WIDTH:2 * GROUP_WIDTH])
    log_a = -LRU_C * r_gate * _softplus(-lam_ref[...])
    a = jnp.exp(log_a)
    u = jnp.sqrt(1.0 - jnp.exp(2.0 * log_a)) * (i_gate * xr)
    row_t = _iota((tc, GROUP_WIDTH), 0)
    d = 1
    while d < tc:
        keep = row_t >= d
        a_sh = jnp.where(keep, pltpu.roll(a, d, axis=0), 1.0)
        u_sh = jnp.where(keep, pltpu.roll(u, d, axis=0), 0.0)
        u = a * u_sh + u
        a = a * a_sh
        d *= 2
    h_lru = u + a * lru_st[0:1, :]
    lru_st[...] = jnp.broadcast_to(h_lru[tc - 1:tc, :], lru_st.shape)
    y = h_lru * _gelu_tanh(proj[:, OFF_LG:OFF_LG + GROUP_WIDTH])
    ybuf[:, 2 * GROUP_WIDTH:3 * GROUP_WIDTH] = y.astype(BF16)

    mixed = _dot(ybuf[...], w_out_ref[...])
    o_ref[0] = _layer_norm(DEEPNORM_ALPHA * x + gate * mixed, ln_w_ref[...], ln_b_ref[...])


def _full_spec(arr):
    nd = arr.ndim
    return pl.BlockSpec(arr.shape, lambda b, s, _nd=nd: (0,) * _nd)


def _mixer_call(x, mod, cos, sin, params):
    bsz, seq, d = x.shape
    tc = MIX_TOKENS
    tok_spec = lambda w: pl.BlockSpec((1, tc, w), lambda b, s: (b, s, 0))
    in_specs = [tok_spec(d),
                pl.BlockSpec((1, 3, d), lambda b, s: (b, 0, 0)),
                tok_spec(LANES), tok_spec(LANES)] + [_full_spec(p) for p in params]
    scratch = [
        pltpu.VMEM((tc, PROJ_W), F32),
        pltpu.VMEM((tc, D_MODEL), BF16),
        pltpu.VMEM((tc + SUBLANES, SSD_XBC), F32),
        pltpu.VMEM((tc + SUBLANES, GROUP_WIDTH), F32),
        pltpu.VMEM((2, SSD_STATE, LANES), F32),
        pltpu.VMEM((2, LANES, LANES), F32),
        pltpu.VMEM((LANES, GROUP_WIDTH), F32),
        pltpu.VMEM((SUBLANES, GROUP_WIDTH), F32),
        pltpu.VMEM((2, CHUNK, GROUP_WIDTH), F32),
        pltpu.VMEM((CHUNK, GROUP_WIDTH), F32),
        pltpu.VMEM((CHUNK, GROUP_WIDTH), F32),
        pltpu.VMEM((SUBLANES, GROUP_WIDTH), F32),
    ]
    return pl.pallas_call(
        _mixer_kernel,
        grid=(bsz, seq // tc),
        in_specs=in_specs,
        out_specs=pl.BlockSpec((1, tc, d), lambda b, s: (b, s, 0)),
        out_shape=jax.ShapeDtypeStruct((bsz, seq, d), F32),
        scratch_shapes=scratch,
        compiler_params=pltpu.CompilerParams(
            dimension_semantics=("arbitrary", "arbitrary"), vmem_limit_bytes=VMEM_LIMIT),
        name="token_mixer",
    )(x, mod, cos, sin, *params)


def _ffn_kernel(x_ref, mod_ref, w_up_ref, w_down_ref, ln_w_ref, ln_b_ref, o_ref, acc_ref):
    x = x_ref[0]
    shift = mod_ref[0, 0:1, :]
    scale = mod_ref[0, 1:2, :]
    gate = mod_ref[0, 2:3, :]
    h_in = (x * (1.0 + scale) + shift).astype(BF16)
    for j in range(D_FF // FFN_BLOCK):
        cols = slice(j * FFN_BLOCK, (j + 1) * FFN_BLOCK)
        g = _dot(h_in, w_up_ref[:, cols])
        u = _dot(h_in, w_up_ref[:, D_FF + j * FFN_BLOCK:D_FF + (j + 1) * FFN_BLOCK])
        act = (_silu(g) * u).astype(BF16)
        part = _dot(act, w_down_ref[cols, :])
        if j == 0:
            acc_ref[...] = part
        else:
            acc_ref[...] += part
    o_ref[0] = _layer_norm(DEEPNORM_ALPHA * x + gate * acc_ref[...], ln_w_ref[...], ln_b_ref[...])


def _ffn_call(x, mod, w_up, w_down, ln_w, ln_b):
    bsz, seq, d = x.shape
    tm = FFN_TOKENS
    return pl.pallas_call(
        _ffn_kernel,
        grid=(bsz, seq // tm),
        in_specs=[pl.BlockSpec((1, tm, d), lambda b, s: (b, s, 0)),
                  pl.BlockSpec((1, 3, d), lambda b, s: (b, 0, 0)),
                  _full_spec(w_up), _full_spec(w_down), _full_spec(ln_w), _full_spec(ln_b)],
        out_specs=pl.BlockSpec((1, tm, d), lambda b, s: (b, s, 0)),
        out_shape=jax.ShapeDtypeStruct((bsz, seq, d), F32),
        scratch_shapes=[pltpu.VMEM((tm, d), F32)],
        compiler_params=pltpu.CompilerParams(
            dimension_semantics=("arbitrary", "arbitrary"), vmem_limit_bytes=VMEM_LIMIT),
        name="swiglu_ffn",
    )(x, mod, w_up, w_down, ln_w, ln_b)


def _proj_columns():
    starts = np.concatenate([[0], np.cumsum(IN_SPLITS)[:-1]])
    (s_z, s_xbc, s_dt, s_rq, s_rk, s_rv, s_rg, s_lg, s_lx, s_gq, s_gk, s_gv, s_glow, s_gr) = starts
    idx = -np.ones((PROJ_W,), np.int64)

    def put(off, start, width):
        idx[off:off + width] = start + np.arange(width)

    put(OFF_Z, s_z, 256)
    put(OFF_XBC, s_xbc, SSD_XBC)
    idx[OFF_DT:OFF_DT + GROUP_WIDTH] = s_dt + np.arange(GROUP_WIDTH) // HEAD_DIM
    put(OFF_RQ, s_rq, 256)
    put(OFF_RK, s_rk, 256)
    put(OFF_RV, s_rv, 256)
    put(OFF_RG, s_rg, 256)
    put(OFF_LG, s_lg, 256)
    put(OFF_LX, s_lx, 256)
    put(OFF_GQ, s_gq, 128)
    put(OFF_GK, s_gk, 128)
    put(OFF_GV, s_gv, 256)
    put(OFF_GR, s_gr, 256)
    put(OFF_GLOW, s_glow, GLA_GATE_RANK)
    return idx


def _block_diag4(w):
    out = jnp.zeros((GROUP_WIDTH, GROUP_WIDTH), w.dtype)
    for i in range(w.shape[0]):
        out = out.at[i * HEAD_DIM:(i + 1) * HEAD_DIM, i * HEAD_DIM:(i + 1) * HEAD_DIM].set(w[i])
    return out


def _row(v):
    return v.reshape(1, -1).astype(F32)


def _rep_heads(v):
    return jnp.repeat(v.astype(F32), HEAD_DIM).reshape(1, GROUP_WIDTH)


def kernel(x, c, positions, w_ada, b_ada, w_in, ssd_conv_w, ssd_conv_b, ssd_dt_bias, ssd_a_log, ssd_d,
           ssd_norm_w, ret_norm_w, lru_conv_w, lru_conv_b, lru_wa, lru_ba, lru_wx, lru_bx, lru_lambda,
           gla_wg2, gla_bg, gla_norm_w, w_out, ln1_w, ln1_b, ffn_w_up, ffn_w_down, ln2_w, ln2_b):
    bsz = x.shape[0]
    depth = w_in.shape[0]
    mod = _ada_call(c, w_ada, b_ada).reshape(depth, bsz, 6, D_MODEL)
    cos, sin = _rope_call(positions)

    idx = _proj_columns()
    col_src = jnp.asarray(np.maximum(idx, 0), jnp.int32)
    col_on = jnp.asarray(idx >= 0)

    for l in range(depth):
        w_in_p = jnp.where(col_on[None, :], jnp.take(w_in[l], col_src, axis=1), 0.0).astype(BF16)
        wg2_p = jnp.zeros((LANES, LANES), F32).at[0:GLA_GATE_RANK, :].set(gla_wg2[l]).astype(BF16)
        lru_w = jnp.concatenate([_block_diag4(lru_wa[l]), _block_diag4(lru_wx[l])], axis=1).astype(BF16)
        params = (
            w_in_p, w_out[l].astype(BF16),
            ssd_conv_w[l], _row(ssd_conv_b[l]), _rep_heads(ssd_dt_bias[l]), _rep_heads(ssd_a_log[l]),
            _rep_heads(ssd_d[l]), _row(ssd_norm_w[l]), _row(ret_norm_w[l]),
            lru_conv_w[l], _row(lru_conv_b[l]), lru_w,
            _row(jnp.concatenate([lru_ba[l], lru_bx[l]])), _row(lru_lambda[l]),
            wg2_p, _row(gla_bg[l]), _row(gla_norm_w[l]), _row(ln1_w[l]), _row(ln1_b[l]),
        )
        x = _mixer_call(x, mod[l, :, 0:3, :], cos, sin, params)
        x = _ffn_call(x, mod[l, :, 3:6, :], ffn_w_up[l].astype(BF16), ffn_w_down[l].astype(BF16),
                      _row(ln2_w[l]), _row(ln2_b[l]))
    return x
```

```python
import functools
import math

import numpy as np
import jax
import jax.numpy as jnp
from jax import lax
from jax.experimental import pallas as pl
from jax.experimental.pallas import tpu as pltpu

F32 = jnp.float32
BF16 = jnp.bfloat16

D_MODEL = 1024
GROUP_WIDTH = 256
HEAD_DIM = 64
N_HEADS = 4
SSD_STATE = 128
SSD_XBC = 768
CONV_K = 4
CHUNK = 128
GLA_CHUNK = 64
GLA_KEY_DIM = 32
GLA_GATE_RANK = 16
GLA_GATE_NORM = 16.0
ROPE_BASE = 10000.0
LRU_C = 8.0
D_FF = 2816
DEPTH = 2
DEEPNORM_ALPHA = (2 * DEPTH) ** 0.25
EPS = 1e-5
IN_SPLITS = (256, 768, 4, 256, 256, 256, 256, 256, 256, 128, 128, 256, 16, 256)

LANES = 128
SUBLANES = 8

OFF_Z = 0
OFF_XBC = 256
OFF_DT = 1024
OFF_RQ = 1280
OFF_RK = 1536
OFF_RV = 1792
OFF_RG = 2048
OFF_LG = 2304
OFF_LX = 2560
OFF_GQ = 2816
OFF_GK = 2944
OFF_GV = 3072
OFF_GR = 3328
OFF_GLOW = 3584
PROJ_W = 3712

MIX_TOKENS = 512
FFN_TOKENS = 512
FFN_BLOCK = 256
VMEM_LIMIT = 56 * 1024 * 1024


def _dot(a, b):
    return jnp.dot(a, b, preferred_element_type=F32)


def _dot_nt(a, b):
    return lax.dot_general(a, b, (((1,), (1,)), ((), ())), preferred_element_type=F32)


def _sigmoid(x):
    return 1.0 / (1.0 + jnp.exp(-x))


def _silu(x):
    return x * _sigmoid(x)


def _softplus(x):
    return jnp.maximum(x, 0.0) + jnp.log(1.0 + jnp.exp(-jnp.abs(x)))


def _gelu_tanh(x):
    return 0.5 * x * (1.0 + jnp.tanh(math.sqrt(2.0 / math.pi) * (x + 0.044715 * (x * x * x))))


def _layer_norm(v, w, b):
    mu = jnp.mean(v, axis=-1, keepdims=True)
    vc = v - mu
    var = jnp.mean(vc * vc, axis=-1, keepdims=True)
    return vc * lax.rsqrt(var + EPS) * w + b


def _split_dot(m_bf16, v):
    hi = v.astype(BF16)
    lo = (v - hi.astype(F32)).astype(BF16)
    return _dot(m_bf16, hi) + _dot(m_bf16, lo)


def _sqrt_nonneg(v):
    return jnp.where(v > 0.0, v * lax.rsqrt(v), 0.0)


def _iota(shape, axis):
    return lax.broadcasted_iota(jnp.int32, shape, axis)


def _block_of(shape, axis, size):
    return _iota(shape, axis) >> (size.bit_length() - 1)


def _within(shape, axis, size):
    return _iota(shape, axis) & (size - 1)


def _ada_kernel(c_ref, w_ref, b_ref, o_ref):
    c_act = _silu(c_ref[...])
    o_ref[0] = jnp.dot(c_act, w_ref[0], preferred_element_type=F32,
                       precision=lax.Precision.HIGHEST) + b_ref[0]


def _ada_call(c, w_ada, b_ada):
    depth, d, n = w_ada.shape
    bsz = c.shape[0]
    nb = n // d
    return pl.pallas_call(
        _ada_kernel,
        grid=(depth, nb),
        in_specs=[pl.BlockSpec((bsz, d), lambda l, j: (0, 0)),
                  pl.BlockSpec((1, d, d), lambda l, j: (l, 0, j)),
                  pl.BlockSpec((1, 1, d), lambda l, j: (l, 0, j))],
        out_specs=pl.BlockSpec((1, bsz, d), lambda l, j: (l, 0, j)),
        out_shape=jax.ShapeDtypeStruct((depth, bsz, n), F32),
        compiler_params=pltpu.CompilerParams(
            dimension_semantics=("arbitrary", "arbitrary"), vmem_limit_bytes=VMEM_LIMIT),
        name="adaln_mod",
    )(c, w_ada, b_ada.reshape(depth, 1, n))


def _rope_kernel(pos_ref, freq_ref, sign_ref, cos_ref, sin_ref):
    n_rows = pos_ref.shape[1]

    def body(r, carry):
        row = pos_ref[0, pl.ds(r, 1), :].astype(F32)
        col = jnp.broadcast_to(row, (LANES, LANES)).T
        ang = col * freq_ref[...]
        start = pl.multiple_of(r * LANES, LANES)
        cos_ref[0, pl.ds(start, LANES), :] = jnp.cos(ang)
        sin_ref[0, pl.ds(start, LANES), :] = jnp.sin(ang) * sign_ref[...]
        return carry

    lax.fori_loop(0, n_rows, body, 0)


def _rope_call(positions):
    bsz, seq = positions.shape
    half = HEAD_DIM // 2
    inv_freq = ROPE_BASE ** (-jnp.arange(half, dtype=F32) / half)
    lane = np.arange(LANES)
    freq = inv_freq[lane % half].reshape(1, LANES)
    sign = jnp.asarray(np.where((lane % HEAD_DIM) < half, -1.0, 1.0), F32).reshape(1, LANES)
    pos3 = positions.reshape(bsz, seq // LANES, LANES)
    out = jax.ShapeDtypeStruct((bsz, seq, LANES), F32)
    return pl.pallas_call(
        _rope_kernel,
        grid=(bsz,),
        in_specs=[pl.BlockSpec((1, seq // LANES, LANES), lambda b: (b, 0, 0)),
                  pl.BlockSpec((1, LANES), lambda b: (0, 0)),
                  pl.BlockSpec((1, LANES), lambda b: (0, 0))],
        out_specs=[pl.BlockSpec((1, seq, LANES), lambda b: (b, 0, 0)),
                   pl.BlockSpec((1, seq, LANES), lambda b: (b, 0, 0))],
        out_shape=[out, out],
        compiler_params=pltpu.CompilerParams(
            dimension_semantics=("arbitrary",), vmem_limit_bytes=VMEM_LIMIT),
        name="rope_table",
    )(pos3, freq, sign)


def _swap_halves(v):
    width = v.shape[1]
    fwd = pltpu.roll(v, HEAD_DIM // 2, axis=1)
    bwd = pltpu.roll(v, width - HEAD_DIM // 2, axis=1)
    first_half = _within(v.shape, 1, HEAD_DIM) < HEAD_DIM // 2
    return jnp.where(first_half, bwd, fwd)


def _causal_conv(ext_ref, n_rows, w_ref, b_ref):
    acc = b_ref[...] + w_ref[0:1, :] * ext_ref[pl.ds(SUBLANES - CONV_K + 1, n_rows), :]
    for k in range(1, CONV_K):
        acc = acc + w_ref[k:k + 1, :] * ext_ref[pl.ds(SUBLANES - CONV_K + 1 + k, n_rows), :]
    return acc


GROUP_BASES = (OFF_Z, OFF_RQ, OFF_LG, OFF_GQ, PROJ_W)


class _GroupedColumns:
    def __init__(self, refs):
        self.refs = refs

    def _locate(self, cols):
        for g, ref in enumerate(self.refs):
            if GROUP_BASES[g] <= cols.start and cols.stop <= GROUP_BASES[g + 1]:
                return ref, slice(cols.start - GROUP_BASES[g], cols.stop - GROUP_BASES[g])
        raise ValueError(f"columns {cols} straddle head groups")

    def __getitem__(self, idx):
        ref, cols = self._locate(idx[1])
        return ref[idx[0], cols]

    def __setitem__(self, idx, value):
        ref, cols = self._locate(idx[1])
        ref[idx[0], cols] = value


def _mixer_kernel(x_ref, mod_ref, x_next_ref, mod_next_ref, cos_ref, sin_ref, w_in_ref, w_out_ref,
                  ssd_cw_ref, ssd_cb_ref, dtb_ref, alog_ref, dskip_ref, ssd_nw_ref, ret_nw_ref,
                  lru_cw_ref, lru_cb_ref, lru_w_ref, lru_b_ref, lam_ref,
                  wg2_ref, bg_ref, gla_nw_ref, ln_w_ref, ln_b_ref,
                  o_ref,
                  proj_ssd, proj_ret, proj_lru, proj_gla, y_ssd, y_ret, y_lru, y_gla,
                  xbc_ext, lx_ext, ssd_st, ret_st, gla_st, lru_st,
                  ret_l, ret_in, ret_end, ret_dec):
    proj = _GroupedColumns((proj_ssd, proj_ret, proj_lru, proj_gla))
    tc = x_ref.shape[1]
    n_chunks = tc // CHUNK
    step = pl.program_id(1)
    n_steps = pl.num_programs(1)
    body_rows = pl.ds(SUBLANES, tc)

    lane128 = _iota((CHUNK, LANES), 1)
    row128 = _iota((CHUNK, LANES), 0)
    low_half = lane128 < HEAD_DIM
    causal = row128 >= lane128
    tril = jnp.where(causal, 1.0, 0.0).astype(BF16)
    sq = (CHUNK, LANES)
    same_gla_chunk = _block_of(sq, 0, GLA_CHUNK) == _block_of(sq, 1, GLA_CHUNK)
    tril_gla = jnp.where(causal, jnp.where(same_gla_chunk, 1.0, 0.0), 0.0).astype(BF16)
    pair_diag = _block_of(sq, 0, HEAD_DIM) == _block_of(sq, 1, HEAD_DIM)

    head_of_lane = _block_of((1, GROUP_WIDTH), 1, HEAD_DIM)
    log_gamma = jnp.zeros((1, GROUP_WIDTH), F32)
    for h in range(N_HEADS):
        log_gamma = jnp.where(head_of_lane == h, math.log1p(-(2.0 ** (-5 - h))), log_gamma)

    def modulated(xr, mr):
        return (xr[0] * (1.0 + mr[0, 1:2, :]) + mr[0, 0:1, :]).astype(BF16)

    def project(h, first, last):
        value = _dot(h, w_in_ref[:, first:last])
        if OFF_XBC <= first and last <= OFF_DT:
            xbc_ext[body_rows, first - OFF_XBC:last - OFF_XBC] = value
        elif OFF_LX <= first and last <= OFF_GQ:
            lx_ext[body_rows, first - OFF_LX:last - OFF_LX] = value
        else:
            proj[:, first:last] = value

    def convolve():
        proj[:, OFF_XBC:OFF_XBC + SSD_XBC] = _silu(_causal_conv(xbc_ext, tc, ssd_cw_ref, ssd_cb_ref))
        xbc_ext[0:SUBLANES, :] = xbc_ext[pl.ds(tc, SUBLANES), :]
        proj[:, OFF_LX:OFF_LX + GROUP_WIDTH] = _causal_conv(lx_ext, tc, lru_cw_ref, lru_cb_ref)
        lx_ext[0:SUBLANES, :] = lx_ext[pl.ds(tc, SUBLANES), :]

    def clear_conv_tails():
        xbc_ext[0:SUBLANES, :] = jnp.zeros((SUBLANES, SSD_XBC), F32)
        lx_ext[0:SUBLANES, :] = jnp.zeros((SUBLANES, GROUP_WIDTH), F32)

    def pieces(h, first, last):
        return [functools.partial(project, h, lo, min(lo + GROUP_WIDTH, last))
                for lo in range(first, last, GROUP_WIDTH)]

    def all_pieces(h):
        return [p for g in range(len(GROUP_BASES) - 1) for p in pieces(h, GROUP_BASES[g], GROUP_BASES[g + 1])]

    @pl.when((pl.program_id(0) == 0) & (step == 0))
    def _first_block():
        clear_conv_tails()
        h_first = modulated(x_ref, mod_ref)
        for piece in all_pieces(h_first):
            piece()
        convolve()

    @pl.when(step == 0)
    def _init():
        ssd_st[...] = jnp.zeros_like(ssd_st)
        ret_st[...] = jnp.zeros_like(ret_st)
        gla_st[...] = jnp.zeros_like(gla_st)
        lru_st[...] = jnp.zeros_like(lru_st)
        t_col = _iota((CHUNK, GROUP_WIDTH), 0).astype(F32)
        ret_in[...] = jnp.exp((t_col + 1.0) * log_gamma)
        ret_end[...] = jnp.exp((CHUNK - 1.0 - t_col) * log_gamma)
        ret_dec[...] = jnp.exp(float(CHUNK) * jnp.broadcast_to(log_gamma, (SUBLANES, GROUP_WIDTH)))
        dist = (row128 - lane128).astype(F32)
        for h in range(N_HEADS):
            lg = math.log1p(-(2.0 ** (-5 - h)))
            ret_l[h // 2, :, (h % 2) * LANES:(h % 2 + 1) * LANES] = jnp.where(
                causal, jnp.exp(dist * lg), 0.0)

    x = x_ref[0]
    gate = mod_ref[0, 2:3, :]
    h_next = modulated(x_next_ref, mod_next_ref)

    gsq = (GROUP_WIDTH, GROUP_WIDTH)
    head_avg = jnp.where(_block_of(gsq, 0, HEAD_DIM) == _block_of(gsq, 1, HEAD_DIM),
                         1.0 / HEAD_DIM, 0.0).astype(BF16)
    gla_k_diag = _block_of((GROUP_WIDTH, LANES), 0, GLA_CHUNK) == _block_of((GROUP_WIDTH, LANES), 1, GLA_KEY_DIM)
    gla_v_diag = _block_of(gsq, 0, GLA_CHUNK) == _block_of(gsq, 1, HEAD_DIM)
    gla_causal = _iota((GLA_CHUNK, GROUP_WIDTH), 0) >= _within((GLA_CHUNK, GROUP_WIDTH), 1, GLA_CHUNK)
    gla_st_diag = (_block_of((LANES, GROUP_WIDTH), 0, GLA_KEY_DIM)
                   == _block_of((LANES, GROUP_WIDTH), 1, HEAD_DIM))
    gla_row_chunk = _block_of((CHUNK, GROUP_WIDTH), 0, GLA_CHUNK)

    def ssd_task(c):
        rows = pl.ds(c * CHUNK, CHUNK)
        dt = _softplus(proj[rows, OFF_DT:OFF_DT + GROUP_WIDTH] + dtb_ref[...])
        log_a = -jnp.exp(alog_ref[...]) * dt
        cum = _split_dot(tril, log_a)
        xs = proj[rows, OFF_XBC:OFF_XBC + GROUP_WIDTH]
        z = proj[rows, OFF_Z:OFF_Z + GROUP_WIDTH]
        xdt = xs * dt
        b_ts, c_gs, scores = [], [], []
        for g in range(2):
            b_g = proj[rows, OFF_XBC + GROUP_WIDTH + g * LANES:OFF_XBC + GROUP_WIDTH + (g + 1) * LANES]
            c_g = proj[rows, OFF_XBC + 2 * GROUP_WIDTH + g * LANES:
                       OFF_XBC + 2 * GROUP_WIDTH + (g + 1) * LANES].astype(BF16)
            b_t = b_g.T.astype(BF16)
            b_ts.append(b_t)
            c_gs.append(c_g)
            scores.append(_dot(c_g, b_t))
        yield
        cum_last = cum[CHUNK - 1:CHUNK, :]
        to_end = jnp.exp(cum_last - cum)
        from_start = jnp.exp(cum)
        y_intra, new_states = [], []
        for g in range(2):
            gl = slice(g * LANES, (g + 1) * LANES)
            cum_g = cum[:, gl]
            cum_sw = pltpu.roll(cum_g, HEAD_DIM, axis=1)
            col_a = jnp.where(low_half, cum_g, cum_sw)
            col_b = jnp.where(low_half, cum_sw, cum_g)
            cum_t = cum_g.T
            l_a = jnp.exp(jnp.where(causal, col_a - cum_t[0:1, :], -jnp.inf))
            l_b = jnp.exp(jnp.where(causal, col_b - cum_t[HEAD_DIM:HEAD_DIM + 1, :], -jnp.inf))
            p = jnp.concatenate([scores[g] * l_a, scores[g] * l_b], axis=1).astype(BF16)
            xdt_g = xdt[:, gl]
            v_bd = jnp.concatenate([jnp.where(low_half, xdt_g, 0.0),
                                    jnp.where(low_half, 0.0, xdt_g)], axis=0).astype(BF16)
            y_intra.append(_dot(p, v_bd))
            new_states.append(_dot(b_ts[g], (xdt_g * to_end[:, gl]).astype(BF16)))
        yield
        y_inter = []
        for g in range(2):
            gl = slice(g * LANES, (g + 1) * LANES)
            prev = ssd_st[g]
            y_inter.append(_dot(c_gs[g], prev.astype(BF16)))
            ssd_st[g] = prev * jnp.exp(cum_last[:, gl]) + new_states[g]
        yield
        y = jnp.concatenate([y_intra[g] + from_start[:, g * LANES:(g + 1) * LANES] * y_inter[g]
                             for g in range(2)], axis=1) + xs * dskip_ref[...]
        y = y * _silu(z)
        y = y * lax.rsqrt(jnp.mean(y * y, axis=-1, keepdims=True) + EPS) * ssd_nw_ref[...]
        y_ssd[rows, :] = y.astype(BF16)

    def ret_task(c):
        rows = pl.ds(c * CHUNK, CHUNK)
        cos = cos_ref[0, rows, :]
        sin = sin_ref[0, rows, :]
        cos2 = jnp.concatenate([cos, cos], axis=1)
        sin2 = jnp.concatenate([sin, sin], axis=1)
        q = proj[rows, OFF_RQ:OFF_RQ + GROUP_WIDTH]
        k = proj[rows, OFF_RK:OFF_RK + GROUP_WIDTH]
        q = q * cos2 + _swap_halves(q) * sin2
        k = (k * cos2 + _swap_halves(k) * sin2) * (HEAD_DIM ** -0.5)
        v = proj[rows, OFF_RV:OFF_RV + GROUP_WIDTH]
        out_gate = proj[rows, OFF_RG:OFF_RG + GROUP_WIDTH]
        v_end = v * ret_end[...]
        q_bf = q.astype(BF16)
        scores, new_states = [], []
        for p_i in range(2):
            pl_ = slice(p_i * LANES, (p_i + 1) * LANES)
            k_t = k[:, pl_].T
            k_bd = jnp.concatenate([jnp.where(row128 < HEAD_DIM, k_t, 0.0),
                                    jnp.where(row128 < HEAD_DIM, 0.0, k_t)], axis=1).astype(BF16)
            scores.append(_dot(q_bf[:, pl_], k_bd))
            new_states.append(_dot(k_t.astype(BF16), v_end[:, pl_].astype(BF16)))
        yield
        y_intra, y_inter = [], []
        for p_i in range(2):
            pl_ = slice(p_i * LANES, (p_i + 1) * LANES)
            p = (scores[p_i] * ret_l[p_i]).astype(BF16)
            v_p = v[:, pl_]
            v_bd = jnp.concatenate([jnp.where(low_half, v_p, 0.0),
                                    jnp.where(low_half, 0.0, v_p)], axis=0).astype(BF16)
            y_intra.append(_dot(p, v_bd))
            prev = ret_st[p_i]
            y_inter.append(_dot(q_bf[:, pl_], prev.astype(BF16)))
            ret_st[p_i] = prev * ret_dec[0:1, pl_] + jnp.where(pair_diag, new_states[p_i], 0.0)
        yield
        y = jnp.concatenate([y_intra[p_i] + ret_in[:, p_i * LANES:(p_i + 1) * LANES] * y_inter[p_i]
                             for p_i in range(2)], axis=1)
        mu = _dot(y.astype(BF16), head_avg)
        yield
        yc = y - mu
        var = _dot((yc * yc).astype(BF16), head_avg)
        yield
        y = yc * lax.rsqrt(var + EPS) * ret_nw_ref[...]
        y_ret[rows, :] = (y * _silu(out_gate)).astype(BF16)

    def lru_task(c):
        rows = pl.ds(c * CHUNK, CHUNK)
        xr = proj[rows, OFF_LX:OFF_LX + GROUP_WIDTH]
        gates = _dot(xr.astype(BF16), lru_w_ref[...])
        out_gate = proj[rows, OFF_LG:OFF_LG + GROUP_WIDTH]
        yield
        gates = gates + lru_b_ref[...]
        r_gate = _sigmoid(gates[:, 0:GROUP_WIDTH])
        i_gate = _sigmoid(gates[:, GROUP_WIDTH:2 * GROUP_WIDTH])
        log_a = -LRU_C * r_gate * _softplus(-lam_ref[...])
        a = jnp.exp(log_a)
        u = _sqrt_nonneg(1.0 - jnp.exp(2.0 * log_a)) * (i_gate * xr)
        n_groups = CHUNK // SUBLANES
        a3 = a.reshape(n_groups, SUBLANES, GROUP_WIDTH)
        u3 = u.reshape(n_groups, SUBLANES, GROUP_WIDTH)
        sub = _iota((1, SUBLANES, GROUP_WIDTH), 1)
        d = 1
        while d < SUBLANES:
            keep = sub >= d
            u3 = jnp.where(keep, a3 * pltpu.roll(u3, d, axis=1) + u3, u3)
            a3 = jnp.where(keep, a3 * pltpu.roll(a3, d, axis=1), a3)
            d *= 2
        h_prev = lru_st[0:1, :]
        gelu_gate = _gelu_tanh(out_gate)
        for grp in range(n_groups):
            h_grp = u3[grp] + a3[grp] * h_prev
            h_prev = h_grp[SUBLANES - 1:SUBLANES, :]
            r0 = c * CHUNK + grp * SUBLANES
            y_lru[r0:r0 + SUBLANES, :] = (
                h_grp * gelu_gate[grp * SUBLANES:(grp + 1) * SUBLANES, :]).astype(BF16)
        lru_st[...] = jnp.broadcast_to(h_prev, lru_st.shape)

    def gla_task(c):
        rows = pl.ds(c * CHUNK, CHUNK)
        g_low = proj[rows, OFF_GLOW:OFF_GLOW + LANES].astype(BF16)
        gate_pre = _dot(g_low, wg2_ref[...])
        q = proj[rows, OFF_GQ:OFF_GQ + LANES] * (GLA_KEY_DIM ** -0.5)
        k = proj[rows, OFF_GK:OFF_GK + LANES]
        v = proj[rows, OFF_GV:OFF_GV + GROUP_WIDTH]
        out_gate = proj[rows, OFF_GR:OFF_GR + GROUP_WIDTH]
        yield
        log_a = -_softplus(-(gate_pre + bg_ref[...])) * (1.0 / GLA_GATE_NORM)
        cum = _split_dot(tril_gla, log_a)
        yield
        first = row128 < GLA_CHUNK
        cum_end = jnp.where(first, cum[GLA_CHUNK - 1:GLA_CHUNK, :], cum[CHUNK - 1:CHUNK, :])
        q_in = (q * jnp.exp(cum)).astype(BF16)
        k_in = k * jnp.exp(-cum)
        k_end_t = (k * jnp.exp(cum_end - cum)).T.astype(BF16)
        dec_t = jnp.exp(cum_end).T
        dec_sw = pltpu.roll(dec_t, GLA_CHUNK, axis=1)
        dec_by_chunk = (jnp.where(low_half, dec_t, dec_sw), jnp.where(low_half, dec_sw, dec_t))
        n_sub = CHUNK // GLA_CHUNK
        scores, new_states = [], []
        for cc in range(n_sub):
            rr = slice(cc * GLA_CHUNK, (cc + 1) * GLA_CHUNK)
            k_rep = jnp.concatenate([k_in[rr, :]] * N_HEADS, axis=0)
            k_bd = jnp.where(gla_k_diag, k_rep, 0.0).astype(BF16)
            scores.append(_dot_nt(q_in[rr, :], k_bd))
            v_only = jnp.where(gla_row_chunk == cc, v, 0.0).astype(BF16)
            new_states.append(_dot(k_end_t, v_only))
        yield
        y_intra, y_inter = [], []
        for cc in range(n_sub):
            rr = slice(cc * GLA_CHUNK, (cc + 1) * GLA_CHUNK)
            masked = jnp.where(gla_causal, scores[cc], 0.0).astype(BF16)
            v_rep = jnp.concatenate([v[rr, :]] * N_HEADS, axis=0)
            v_bd = jnp.where(gla_v_diag, v_rep, 0.0).astype(BF16)
            y_intra.append(_dot(masked, v_bd))
            prev = gla_st[...]
            y_inter.append(_dot(q_in[rr, :], prev.astype(BF16)))
            dec = jnp.concatenate([dec_by_chunk[cc]] * 2, axis=1)
            gla_st[...] = prev * dec + jnp.where(gla_st_diag, new_states[cc], 0.0)
        yield
        y = jnp.concatenate([y_intra[cc] + y_inter[cc] for cc in range(n_sub)], axis=0)
        ms = _dot((y * y).astype(BF16), head_avg)
        yield
        y = y * lax.rsqrt(ms + EPS) * gla_nw_ref[...]
        y_gla[rows, :] = (y * _silu(out_gate)).astype(BF16)

    tasks = [task(c) for task in (ssd_task, ret_task, lru_task, gla_task) for c in range(n_chunks)]
    fillers = all_pieces(h_next)
    n_rounds = 6
    per_round = -(-len(fillers) // (n_rounds - 1))
    while tasks:
        alive = []
        for task in tasks:
            try:
                next(task)
                alive.append(task)
            except StopIteration:
                pass
        tasks = alive
        for filler in fillers[:per_round]:
            filler()
        fillers = fillers[per_round:]
    for filler in fillers:
        filler()

    y_refs = (y_ssd, y_ret, y_lru, y_gla)
    out_parts = [_dot(y_refs[g][...], w_out_ref[g * GROUP_WIDTH:(g + 1) * GROUP_WIDTH, :])
                 for g in range(len(y_refs))]

    keep_tail = jnp.where(step == n_steps - 1, 0.0, 1.0)
    xbc_ext[0:SUBLANES, :] = xbc_ext[0:SUBLANES, :] * keep_tail
    lx_ext[0:SUBLANES, :] = lx_ext[0:SUBLANES, :] * keep_tail
    convolve()
    mixed = out_parts[0] + out_parts[1] + out_parts[2] + out_parts[3]
    o_ref[0] = _layer_norm(DEEPNORM_ALPHA * x + gate * mixed, ln_w_ref[...], ln_b_ref[...])


def _full_spec(arr):
    nd = arr.ndim
    return pl.BlockSpec(arr.shape, lambda b, s, _nd=nd: (0,) * _nd)


def _mixer_call(x, mod, cos, sin, params):
    bsz, seq, d = x.shape
    tc = MIX_TOKENS
    n_steps = seq // tc
    tok_spec = lambda w: pl.BlockSpec((1, tc, w), lambda b, s: (b, s, 0))

    def next_batch(b, s):
        return jnp.minimum(b + (s + 1) // n_steps, bsz - 1)

    in_specs = [tok_spec(d),
                pl.BlockSpec((1, 3, d), lambda b, s: (b, 0, 0)),
                pl.BlockSpec((1, tc, d), lambda b, s: (next_batch(b, s), (s + 1) % n_steps, 0)),
                pl.BlockSpec((1, 3, d), lambda b, s: (next_batch(b, s), 0, 0)),
                tok_spec(LANES), tok_spec(LANES)] + [_full_spec(p) for p in params]
    scratch = [
        pltpu.VMEM((tc, OFF_RQ - OFF_Z), F32),
        pltpu.VMEM((tc, OFF_LG - OFF_RQ), F32),
        pltpu.VMEM((tc, OFF_GQ - OFF_LG), F32),
        pltpu.VMEM((tc, PROJ_W - OFF_GQ), F32),
        pltpu.VMEM((tc, GROUP_WIDTH), BF16),
        pltpu.VMEM((tc, GROUP_WIDTH), BF16),
        pltpu.VMEM((tc, GROUP_WIDTH), BF16),
        pltpu.VMEM((tc, GROUP_WIDTH), BF16),
        pltpu.VMEM((tc + SUBLANES, SSD_XBC), F32),
        pltpu.VMEM((tc + SUBLANES, GROUP_WIDTH), F32),
        pltpu.VMEM((2, SSD_STATE, LANES), F32),
        pltpu.VMEM((2, LANES, LANES), F32),
        pltpu.VMEM((LANES, GROUP_WIDTH), F32),
        pltpu.VMEM((SUBLANES, GROUP_WIDTH), F32),
        pltpu.VMEM((2, CHUNK, GROUP_WIDTH), F32),
        pltpu.VMEM((CHUNK, GROUP_WIDTH), F32),
        pltpu.VMEM((CHUNK, GROUP_WIDTH), F32),
        pltpu.VMEM((SUBLANES, GROUP_WIDTH), F32),
    ]
    return pl.pallas_call(
        _mixer_kernel,
        grid=(bsz, seq // tc),
        in_specs=in_specs,
        out_specs=pl.BlockSpec((1, tc, d), lambda b, s: (b, s, 0)),
        out_shape=jax.ShapeDtypeStruct((bsz, seq, d), F32),
        scratch_shapes=scratch,
        compiler_params=pltpu.CompilerParams(
            dimension_semantics=("arbitrary", "arbitrary"), vmem_limit_bytes=VMEM_LIMIT),
        name="token_mixer",
    )(x, mod, x, mod, cos, sin, *params)


def _ffn_kernel(x_ref, mod_ref, w_up_ref, w_down_ref, ln_w_ref, ln_b_ref, o_ref, acc_ref):
    x = x_ref[0]
    shift = mod_ref[0, 0:1, :]
    scale = mod_ref[0, 1:2, :]
    gate = mod_ref[0, 2:3, :]
    h_in = (x * (1.0 + scale) + shift).astype(BF16)
    for j in range(D_FF // FFN_BLOCK):
        cols = slice(j * FFN_BLOCK, (j + 1) * FFN_BLOCK)
        g = _dot(h_in, w_up_ref[:, cols])
        u = _dot(h_in, w_up_ref[:, D_FF + j * FFN_BLOCK:D_FF + (j + 1) * FFN_BLOCK])
        act = (_silu(g) * u).astype(BF16)
        part = _dot(act, w_down_ref[cols, :])
        if j == 0:
            acc_ref[...] = part
        else:
            acc_ref[...] += part
    o_ref[0] = _layer_norm(DEEPNORM_ALPHA * x + gate * acc_ref[...], ln_w_ref[...], ln_b_ref[...])


def _ffn_call(x, mod, w_up, w_down, ln_w, ln_b):
    bsz, seq, d = x.shape
    tm = FFN_TOKENS
    return pl.pallas_call(
        _ffn_kernel,
        grid=(bsz, seq // tm),
        in_specs=[pl.BlockSpec((1, tm, d), lambda b, s: (b, s, 0)),
                  pl.BlockSpec((1, 3, d), lambda b, s: (b, 0, 0)),
                  _full_spec(w_up), _full_spec(w_down), _full_spec(ln_w), _full_spec(ln_b)],
        out_specs=pl.BlockSpec((1, tm, d), lambda b, s: (b, s, 0)),
        out_shape=jax.ShapeDtypeStruct((bsz, seq, d), F32),
        scratch_shapes=[pltpu.VMEM((tm, d), F32)],
        compiler_params=pltpu.CompilerParams(
            dimension_semantics=("arbitrary", "arbitrary"), vmem_limit_bytes=VMEM_LIMIT),
        name="swiglu_ffn",
    )(x, mod, w_up, w_down, ln_w, ln_b)


def _proj_columns():
    starts = np.concatenate([[0], np.cumsum(IN_SPLITS)[:-1]])
    (s_z, s_xbc, s_dt, s_rq, s_rk, s_rv, s_rg, s_lg, s_lx, s_gq, s_gk, s_gv, s_glow, s_gr) = starts
    idx = -np.ones((PROJ_W,), np.int64)

    def put(off, start, width):
        idx[off:off + width] = start + np.arange(width)

    put(OFF_Z, s_z, 256)
    put(OFF_XBC, s_xbc, SSD_XBC)
    idx[OFF_DT:OFF_DT + GROUP_WIDTH] = s_dt + np.arange(GROUP_WIDTH) // HEAD_DIM
    put(OFF_RQ, s_rq, 256)
    put(OFF_RK, s_rk, 256)
    put(OFF_RV, s_rv, 256)
    put(OFF_RG, s_rg, 256)
    put(OFF_LG, s_lg, 256)
    put(OFF_LX, s_lx, 256)
    put(OFF_GQ, s_gq, 128)
    put(OFF_GK, s_gk, 128)
    put(OFF_GV, s_gv, 256)
    put(OFF_GR, s_gr, 256)
    put(OFF_GLOW, s_glow, GLA_GATE_RANK)
    return idx


def _block_diag4(w):
    out = jnp.zeros((GROUP_WIDTH, GROUP_WIDTH), w.dtype)
    for i in range(w.shape[0]):
        out = out.at[i * HEAD_DIM:(i + 1) * HEAD_DIM, i * HEAD_DIM:(i + 1) * HEAD_DIM].set(w[i])
    return out


def _row(v):
    return v.reshape(1, -1).astype(F32)


def _rep_heads(v):
    return jnp.repeat(v.astype(F32), HEAD_DIM).reshape(1, GROUP_WIDTH)


def kernel(x, c, positions, w_ada, b_ada, w_in, ssd_conv_w, ssd_conv_b, ssd_dt_bias, ssd_a_log, ssd_d,
           ssd_norm_w, ret_norm_w, lru_conv_w, lru_conv_b, lru_wa, lru_ba, lru_wx, lru_bx, lru_lambda,
           gla_wg2, gla_bg, gla_norm_w, w_out, ln1_w, ln1_b, ffn_w_up, ffn_w_down, ln2_w, ln2_b):
    bsz = x.shape[0]
    depth = w_in.shape[0]
    mod = _ada_call(c, w_ada, b_ada).reshape(depth, bsz, 6, D_MODEL)
    cos, sin = _rope_call(positions)

    idx = _proj_columns()
    col_src = jnp.asarray(np.maximum(idx, 0), jnp.int32)
    col_on = jnp.asarray(idx >= 0)

    for l in range(depth):
        w_in_p = jnp.where(col_on[None, :], jnp.take(w_in[l], col_src, axis=1), 0.0).astype(BF16)
        wg2_p = jnp.zeros((LANES, LANES), F32).at[0:GLA_GATE_RANK, :].set(gla_wg2[l]).astype(BF16)
        lru_w = jnp.concatenate([_block_diag4(lru_wa[l]), _block_diag4(lru_wx[l])], axis=1).astype(BF16)
        params = (
            w_in_p, w_out[l].astype(BF16),
            ssd_conv_w[l], _row(ssd_conv_b[l]), _rep_heads(ssd_dt_bias[l]), _rep_heads(ssd_a_log[l]),
            _rep_heads(ssd_d[l]), _row(ssd_norm_w[l]), _row(ret_norm_w[l]),
            lru_conv_w[l], _row(lru_conv_b[l]), lru_w,
            _row(jnp.concatenate([lru_ba[l], lru_bx[l]])), _row(lru_lambda[l]),
            wg2_p, _row(gla_bg[l]), _row(gla_norm_w[l]), _row(ln1_w[l]), _row(ln1_b[l]),
        )
        x = _mixer_call(x, mod[l, :, 0:3, :], cos, sin, params)
        x = _ffn_call(x, mod[l, :, 3:6, :], ffn_w_up[l].astype(BF16), ffn_w_down[l].astype(BF16),
                      _row(ln2_w[l]), _row(ln2_b[l]))
    return x
```

```python
import functools
import math
from typing import NamedTuple

import numpy as np
import jax
import jax.numpy as jnp
from jax import lax
from jax.experimental import pallas as pl
from jax.experimental.pallas import tpu as pltpu

F32 = jnp.float32
BF16 = jnp.bfloat16

D_MODEL = 1024
GROUP_WIDTH = 256
HEAD_DIM = 64
N_HEADS = 4
SSD_STATE = 128
SSD_XBC = 768
CONV_K = 4
CHUNK = 128
GLA_CHUNK = 64
GLA_KEY_DIM = 32
GLA_GATE_RANK = 16
GLA_GATE_NORM = 16.0
ROPE_BASE = 10000.0
LRU_C = 8.0
D_FF = 2816
DEPTH = 2
DEEPNORM_ALPHA = (2 * DEPTH) ** 0.25
EPS = 1e-5
IN_SPLITS = (256, 768, 4, 256, 256, 256, 256, 256, 256, 128, 128, 256, 16, 256)

LANES = 128
SUBLANES = 8

OFF_Z = 0
OFF_XBC = 256
OFF_DT = 1024
OFF_RQ = 1280
OFF_RK = 1536
OFF_RV = 1792
OFF_RG = 2048
OFF_LG = 2304
OFF_LX = 2560
OFF_GQ = 2816
OFF_GK = 2944
OFF_GV = 3072
OFF_GR = 3328
OFF_GLOW = 3584
PROJ_W = 3712

MIX_TOKENS = 512
FFN_TOKENS = 512
FFN_BLOCK = 256
FFN_LN_STRIPS = 8
VMEM_LIMIT = 56 * 1024 * 1024


def _dot(a, b):
    return jnp.dot(a, b, preferred_element_type=F32)


def _dot_nt(a, b):
    return lax.dot_general(a, b, (((1,), (1,)), ((), ())), preferred_element_type=F32)


def _sigmoid(x):
    return 1.0 / (1.0 + jnp.exp(-x))


def _silu(x):
    return x * _sigmoid(x)


def _softplus(x):
    return jnp.maximum(x, 0.0) + jnp.log(1.0 + jnp.exp(-jnp.abs(x)))


def _gelu_tanh(x):
    return 0.5 * x * (1.0 + jnp.tanh(math.sqrt(2.0 / math.pi) * (x + 0.044715 * (x * x * x))))


def _layer_norm(v, w, b):
    mu = jnp.mean(v, axis=-1, keepdims=True)
    vc = v - mu
    var = jnp.mean(vc * vc, axis=-1, keepdims=True)
    return vc * lax.rsqrt(var + EPS) * w + b


def _split_dot(m_bf16, v):
    hi = v.astype(BF16)
    lo = (v - hi.astype(F32)).astype(BF16)
    return _dot(m_bf16, hi) + _dot(m_bf16, lo)


def _sqrt_nonneg(v):
    return jnp.where(v > 0.0, v * lax.rsqrt(v), 0.0)


def _iota(shape, axis):
    return lax.broadcasted_iota(jnp.int32, shape, axis)


def _block_of(shape, axis, size):
    return _iota(shape, axis) >> (size.bit_length() - 1)


def _within(shape, axis, size):
    return _iota(shape, axis) & (size - 1)


def _ada_kernel(c_ref, w_ref, b_ref, o_ref):
    c_act = _silu(c_ref[...])
    o_ref[0] = jnp.dot(c_act, w_ref[0], preferred_element_type=F32,
                       precision=lax.Precision.HIGHEST) + b_ref[0]


def _ada_call(c, w_ada, b_ada):
    depth, d, n = w_ada.shape
    bsz = c.shape[0]
    nb = n // d
    return pl.pallas_call(
        _ada_kernel,
        grid=(depth, nb),
        in_specs=[pl.BlockSpec((bsz, d), lambda l, j: (0, 0)),
                  pl.BlockSpec((1, d, d), lambda l, j: (l, 0, j)),
                  pl.BlockSpec((1, 1, d), lambda l, j: (l, 0, j))],
        out_specs=pl.BlockSpec((1, bsz, d), lambda l, j: (l, 0, j)),
        out_shape=jax.ShapeDtypeStruct((depth, bsz, n), F32),
        compiler_params=pltpu.CompilerParams(
            dimension_semantics=("arbitrary", "arbitrary"), vmem_limit_bytes=VMEM_LIMIT),
        name="adaln_mod",
    )(c, w_ada, b_ada.reshape(depth, 1, n))


def _rope_kernel(pos_ref, freq_ref, sign_ref, cos_ref, sin_ref):
    n_rows = pos_ref.shape[1]

    half = HEAD_DIM // 2
    n_groups = LANES // half
    lane_group = _block_of((half, LANES), 1, half)

    def by_lane_group(tiles):
        out = tiles[n_groups - 1]
        for g in range(n_groups - 2, -1, -1):
            out = jnp.where(lane_group == g, tiles[g], out)
        return out

    def body(r, carry):
        row = pos_ref[0, pl.ds(r, 1), :].astype(F32)
        col = jnp.broadcast_to(row, (LANES, LANES)).T
        packed = by_lane_group([col[q * half:(q + 1) * half, :] for q in range(n_groups)])
        ang = packed * freq_ref[...]
        start = pl.multiple_of(r * LANES, LANES)
        for table, out_ref, scale in ((jnp.cos(ang), cos_ref, None), (jnp.sin(ang), sin_ref, sign_ref[...])):
            shifted = [table] + [pltpu.roll(table, k * half, axis=1) for k in range(1, n_groups)]
            for q in range(n_groups):
                full = by_lane_group([shifted[(g - q) % n_groups] for g in range(n_groups)])
                if scale is not None:
                    full = full * scale
                out_ref[0, pl.ds(start + q * half, half), :] = full
        return carry

    lax.fori_loop(0, n_rows, body, 0)


def _rope_call(positions):
    bsz, seq = positions.shape
    half = HEAD_DIM // 2
    inv_freq = ROPE_BASE ** (-jnp.arange(half, dtype=F32) / half)
    lane = np.arange(LANES)
    freq = inv_freq[lane % half].reshape(1, LANES)
    sign = jnp.asarray(np.where((lane % HEAD_DIM) < half, -1.0, 1.0), F32).reshape(1, LANES)
    pos3 = positions.reshape(bsz, seq // LANES, LANES)
    out = jax.ShapeDtypeStruct((bsz, seq, LANES), F32)
    return pl.pallas_call(
        _rope_kernel,
        grid=(bsz,),
        in_specs=[pl.BlockSpec((1, seq // LANES, LANES), lambda b: (b, 0, 0)),
                  pl.BlockSpec((1, LANES), lambda b: (0, 0)),
                  pl.BlockSpec((1, LANES), lambda b: (0, 0))],
        out_specs=[pl.BlockSpec((1, seq, LANES), lambda b: (b, 0, 0)),
                   pl.BlockSpec((1, seq, LANES), lambda b: (b, 0, 0))],
        out_shape=[out, out],
        compiler_params=pltpu.CompilerParams(
            dimension_semantics=("arbitrary",), vmem_limit_bytes=VMEM_LIMIT),
        name="rope_table",
    )(pos3, freq, sign)


def _swap_halves(v):
    width = v.shape[1]
    fwd = pltpu.roll(v, HEAD_DIM // 2, axis=1)
    bwd = pltpu.roll(v, width - HEAD_DIM // 2, axis=1)
    first_half = _within(v.shape, 1, HEAD_DIM) < HEAD_DIM // 2
    return jnp.where(first_half, bwd, fwd)


def _causal_conv(ext_ref, row0, n_rows, w_ref, b_ref):
    first = row0 + SUBLANES - CONV_K + 1
    acc = b_ref[...] + w_ref[0:1, :] * ext_ref[pl.ds(first, n_rows), :]
    for k in range(1, CONV_K):
        acc = acc + w_ref[k:k + 1, :] * ext_ref[pl.ds(first + k, n_rows), :]
    return acc


GROUP_BASES = (OFF_Z, OFF_RQ, OFF_LG, OFF_GQ, PROJ_W)


class _GroupedColumns:
    def __init__(self, refs):
        self.refs = refs

    def _locate(self, cols):
        for g, ref in enumerate(self.refs):
            if GROUP_BASES[g] <= cols.start and cols.stop <= GROUP_BASES[g + 1]:
                return ref, slice(cols.start - GROUP_BASES[g], cols.stop - GROUP_BASES[g])
        raise ValueError(f"columns {cols} straddle head groups")

    def __getitem__(self, idx):
        ref, cols = self._locate(idx[1])
        return ref[idx[0], cols]

    def __setitem__(self, idx, value):
        ref, cols = self._locate(idx[1])
        ref[idx[0], cols] = value


def _mixer_kernel(x_ref, mod_ref, x_next_ref, mod_next_ref, cos_ref, sin_ref, w_in_ref, w_out_ref,
                  ssd_cw_ref, ssd_cb_ref, dtb_ref, alog_ref, dskip_ref, ssd_nw_ref, ret_nw_ref,
                  lru_cw_ref, lru_cb_ref, lru_w_ref, lru_b_ref, lam_ref,
                  wg2_ref, bg_ref, gla_nw_ref, ln_w_ref, ln_b_ref,
                  o_ref,
                  proj_ssd, proj_ret, proj_lru, proj_gla, y_ssd, y_ret, y_lru, y_gla,
                  xbc_ext, lx_ext, ssd_st, ret_st, gla_st, lru_st,
                  ret_l, ret_in, ret_end, ret_dec):
    proj = _GroupedColumns((proj_ssd, proj_ret, proj_lru, proj_gla))
    tc = x_ref.shape[1]
    n_chunks = tc // CHUNK
    step = pl.program_id(1)
    n_steps = pl.num_programs(1)
    body_rows = pl.ds(SUBLANES, tc)

    lane128 = _iota((CHUNK, LANES), 1)
    row128 = _iota((CHUNK, LANES), 0)
    low_half = lane128 < HEAD_DIM
    causal = row128 >= lane128
    tril = jnp.where(causal, 1.0, 0.0).astype(BF16)
    sq = (CHUNK, LANES)
    same_gla_chunk = _block_of(sq, 0, GLA_CHUNK) == _block_of(sq, 1, GLA_CHUNK)
    tril_gla = jnp.where(causal, jnp.where(same_gla_chunk, 1.0, 0.0), 0.0).astype(BF16)
    pair_diag = _block_of(sq, 0, HEAD_DIM) == _block_of(sq, 1, HEAD_DIM)

    head_of_lane = _block_of((1, GROUP_WIDTH), 1, HEAD_DIM)
    log_gamma = jnp.zeros((1, GROUP_WIDTH), F32)
    for h in range(N_HEADS):
        log_gamma = jnp.where(head_of_lane == h, math.log1p(-(2.0 ** (-5 - h))), log_gamma)

    def modulated(xr, mr):
        return (xr[0] * (1.0 + mr[0, 1:2, :]) + mr[0, 0:1, :]).astype(BF16)

    def project(h, first, last):
        value = _dot(h, w_in_ref[:, first:last])
        if OFF_XBC <= first and last <= OFF_DT:
            xbc_ext[body_rows, first - OFF_XBC:last - OFF_XBC] = value
        elif OFF_LX <= first and last <= OFF_GQ:
            lx_ext[body_rows, first - OFF_LX:last - OFF_LX] = value
        else:
            proj[:, first:last] = value

    def convolve():
        proj[:, OFF_XBC:OFF_XBC + SSD_XBC] = _silu(_causal_conv(xbc_ext, 0, tc, ssd_cw_ref, ssd_cb_ref))
        xbc_ext[0:SUBLANES, :] = xbc_ext[pl.ds(tc, SUBLANES), :]
        proj[:, OFF_LX:OFF_LX + GROUP_WIDTH] = _causal_conv(lx_ext, 0, tc, lru_cw_ref, lru_cb_ref)
        lx_ext[0:SUBLANES, :] = lx_ext[pl.ds(tc, SUBLANES), :]

    def clear_conv_tails():
        xbc_ext[0:SUBLANES, :] = jnp.zeros((SUBLANES, SSD_XBC), F32)
        lx_ext[0:SUBLANES, :] = jnp.zeros((SUBLANES, GROUP_WIDTH), F32)

    def pieces(h, first, last):
        return [functools.partial(project, h, lo, min(lo + GROUP_WIDTH, last))
                for lo in range(first, last, GROUP_WIDTH)]

    def all_pieces(h):
        return [p for g in range(len(GROUP_BASES) - 1) for p in pieces(h, GROUP_BASES[g], GROUP_BASES[g + 1])]

    @pl.when((pl.program_id(0) == 0) & (step == 0))
    def _first_block():
        clear_conv_tails()
        h_first = modulated(x_ref, mod_ref)
        for piece in all_pieces(h_first):
            piece()
        convolve()

    @pl.when(step == 0)
    def _init():
        ssd_st[...] = jnp.zeros_like(ssd_st)
        ret_st[...] = jnp.zeros_like(ret_st)
        gla_st[...] = jnp.zeros_like(gla_st)
        lru_st[...] = jnp.zeros_like(lru_st)
        t_col = _iota((CHUNK, GROUP_WIDTH), 0).astype(F32)
        ret_in[...] = jnp.exp((t_col + 1.0) * log_gamma)
        ret_end[...] = jnp.exp((CHUNK - 1.0 - t_col) * log_gamma)
        ret_dec[...] = jnp.exp(float(CHUNK) * jnp.broadcast_to(log_gamma, (SUBLANES, GROUP_WIDTH)))
        dist = (row128 - lane128).astype(F32)
        for h in range(N_HEADS):
            lg = math.log1p(-(2.0 ** (-5 - h)))
            ret_l[h // 2, :, (h % 2) * LANES:(h % 2 + 1) * LANES] = jnp.where(
                causal, jnp.exp(dist * lg), 0.0)

    gsq = (GROUP_WIDTH, GROUP_WIDTH)
    head_avg = jnp.where(_block_of(gsq, 0, HEAD_DIM) == _block_of(gsq, 1, HEAD_DIM),
                         1.0 / HEAD_DIM, 0.0).astype(BF16)
    gla_k_diag = _block_of((GROUP_WIDTH, LANES), 0, GLA_CHUNK) == _block_of((GROUP_WIDTH, LANES), 1, GLA_KEY_DIM)
    gla_v_diag = _block_of(gsq, 0, GLA_CHUNK) == _block_of(gsq, 1, HEAD_DIM)
    gla_causal = _iota((GLA_CHUNK, GROUP_WIDTH), 0) >= _within((GLA_CHUNK, GROUP_WIDTH), 1, GLA_CHUNK)
    gla_st_diag = (_block_of((LANES, GROUP_WIDTH), 0, GLA_KEY_DIM)
                   == _block_of((LANES, GROUP_WIDTH), 1, HEAD_DIM))
    gla_row_chunk = _block_of((CHUNK, GROUP_WIDTH), 0, GLA_CHUNK)

    def ssd_task(c):
        rows = pl.ds(c * CHUNK, CHUNK)
        dt = _softplus(proj[rows, OFF_DT:OFF_DT + GROUP_WIDTH] + dtb_ref[...])
        log_a = -jnp.exp(alog_ref[...]) * dt
        cum = _split_dot(tril, log_a)
        xs = proj[rows, OFF_XBC:OFF_XBC + GROUP_WIDTH]
        z = proj[rows, OFF_Z:OFF_Z + GROUP_WIDTH]
        xdt = xs * dt
        b_ts, c_gs, scores = [], [], []
        for g in range(2):
            b_g = proj[rows, OFF_XBC + GROUP_WIDTH + g * LANES:OFF_XBC + GROUP_WIDTH + (g + 1) * LANES]
            c_g = proj[rows, OFF_XBC + 2 * GROUP_WIDTH + g * LANES:
                       OFF_XBC + 2 * GROUP_WIDTH + (g + 1) * LANES].astype(BF16)
            b_t = b_g.T.astype(BF16)
            b_ts.append(b_t)
            c_gs.append(c_g)
            scores.append(_dot(c_g, b_t))
        yield
        cum_last = cum[CHUNK - 1:CHUNK, :]
        to_end = jnp.exp(cum_last - cum)
        from_start = jnp.exp(cum)
        y_intra, new_states = [], []
        for g in range(2):
            gl = slice(g * LANES, (g + 1) * LANES)
            cum_g = cum[:, gl]
            cum_sw = pltpu.roll(cum_g, HEAD_DIM, axis=1)
            col_a = jnp.where(low_half, cum_g, cum_sw)
            col_b = jnp.where(low_half, cum_sw, cum_g)
            cum_t = cum_g.T
            l_a = jnp.exp(jnp.where(causal, col_a - cum_t[0:1, :], -jnp.inf))
            l_b = jnp.exp(jnp.where(causal, col_b - cum_t[HEAD_DIM:HEAD_DIM + 1, :], -jnp.inf))
            p = jnp.concatenate([scores[g] * l_a, scores[g] * l_b], axis=1).astype(BF16)
            xdt_g = xdt[:, gl]
            v_bd = jnp.concatenate([jnp.where(low_half, xdt_g, 0.0),
                                    jnp.where(low_half, 0.0, xdt_g)], axis=0).astype(BF16)
            y_intra.append(_dot(p, v_bd))
            new_states.append(_dot(b_ts[g], (xdt_g * to_end[:, gl]).astype(BF16)))
        yield
        y_inter = []
        for g in range(2):
            gl = slice(g * LANES, (g + 1) * LANES)
            prev = ssd_st[g]
            y_inter.append(_dot(c_gs[g], prev.astype(BF16)))
            ssd_st[g] = prev * jnp.exp(cum_last[:, gl]) + new_states[g]
        yield
        y = jnp.concatenate([y_intra[g] + from_start[:, g * LANES:(g + 1) * LANES] * y_inter[g]
                             for g in range(2)], axis=1) + xs * dskip_ref[...]
        y = y * _silu(z)
        y = y * lax.rsqrt(jnp.mean(y * y, axis=-1, keepdims=True) + EPS) * ssd_nw_ref[...]
        y_ssd[rows, :] = y.astype(BF16)

    def ret_task(c):
        rows = pl.ds(c * CHUNK, CHUNK)
        cos = cos_ref[0, rows, :]
        sin = sin_ref[0, rows, :]
        cos2 = jnp.concatenate([cos, cos], axis=1)
        sin2 = jnp.concatenate([sin, sin], axis=1)
        q = proj[rows, OFF_RQ:OFF_RQ + GROUP_WIDTH]
        k = proj[rows, OFF_RK:OFF_RK + GROUP_WIDTH]
        q = q * cos2 + _swap_halves(q) * sin2
        k = (k * cos2 + _swap_halves(k) * sin2) * (HEAD_DIM ** -0.5)
        v = proj[rows, OFF_RV:OFF_RV + GROUP_WIDTH]
        out_gate = proj[rows, OFF_RG:OFF_RG + GROUP_WIDTH]
        v_end = v * ret_end[...]
        q_bf = q.astype(BF16)
        scores, new_states = [], []
        for p_i in range(2):
            pl_ = slice(p_i * LANES, (p_i + 1) * LANES)
            k_t = k[:, pl_].T
            k_bd = jnp.concatenate([jnp.where(row128 < HEAD_DIM, k_t, 0.0),
                                    jnp.where(row128 < HEAD_DIM, 0.0, k_t)], axis=1).astype(BF16)
            scores.append(_dot(q_bf[:, pl_], k_bd))
            new_states.append(_dot(k_t.astype(BF16), v_end[:, pl_].astype(BF16)))
        yield
        y_intra, y_inter = [], []
        for p_i in range(2):
            pl_ = slice(p_i * LANES, (p_i + 1) * LANES)
            p = (scores[p_i] * ret_l[p_i]).astype(BF16)
            v_p = v[:, pl_]
            v_bd = jnp.concatenate([jnp.where(low_half, v_p, 0.0),
                                    jnp.where(low_half, 0.0, v_p)], axis=0).astype(BF16)
            y_intra.append(_dot(p, v_bd))
            prev = ret_st[p_i]
            y_inter.append(_dot(q_bf[:, pl_], prev.astype(BF16)))
            ret_st[p_i] = prev * ret_dec[0:1, pl_] + jnp.where(pair_diag, new_states[p_i], 0.0)
        yield
        y = jnp.concatenate([y_intra[p_i] + ret_in[:, p_i * LANES:(p_i + 1) * LANES] * y_inter[p_i]
                             for p_i in range(2)], axis=1)
        mu = _dot(y.astype(BF16), head_avg)
        yield
        yc = y - mu
        var = _dot((yc * yc).astype(BF16), head_avg)
        yield
        y = yc * lax.rsqrt(var + EPS) * ret_nw_ref[...]
        y_ret[rows, :] = (y * _silu(out_gate)).astype(BF16)

    def lru_task(c):
        rows = pl.ds(c * CHUNK, CHUNK)
        xr = proj[rows, OFF_LX:OFF_LX + GROUP_WIDTH]
        gates = _dot(xr.astype(BF16), lru_w_ref[...])
        out_gate = proj[rows, OFF_LG:OFF_LG + GROUP_WIDTH]
        yield
        gates = gates + lru_b_ref[...]
        r_gate = _sigmoid(gates[:, 0:GROUP_WIDTH])
        i_gate = _sigmoid(gates[:, GROUP_WIDTH:2 * GROUP_WIDTH])
        log_a = -LRU_C * r_gate * _softplus(-lam_ref[...])
        a = jnp.exp(log_a)
        u = _sqrt_nonneg(1.0 - jnp.exp(2.0 * log_a)) * (i_gate * xr)
        n_groups = CHUNK // SUBLANES
        a3 = a.reshape(n_groups, SUBLANES, GROUP_WIDTH)
        u3 = u.reshape(n_groups, SUBLANES, GROUP_WIDTH)
        sub = _iota((1, SUBLANES, GROUP_WIDTH), 1)
        d = 1
        while d < SUBLANES:
            keep = sub >= d
            u3 = jnp.where(keep, a3 * pltpu.roll(u3, d, axis=1) + u3, u3)
            a3 = jnp.where(keep, a3 * pltpu.roll(a3, d, axis=1), a3)
            d *= 2
        h_prev = lru_st[0:1, :]
        gelu_gate = _gelu_tanh(out_gate)
        for grp in range(n_groups):
            h_grp = u3[grp] + a3[grp] * h_prev
            h_prev = h_grp[SUBLANES - 1:SUBLANES, :]
            r0 = c * CHUNK + grp * SUBLANES
            y_lru[r0:r0 + SUBLANES, :] = (
                h_grp * gelu_gate[grp * SUBLANES:(grp + 1) * SUBLANES, :]).astype(BF16)
        lru_st[...] = jnp.broadcast_to(h_prev, lru_st.shape)

    def gla_task(c):
        rows = pl.ds(c * CHUNK, CHUNK)
        g_low = proj[rows, OFF_GLOW:OFF_GLOW + LANES].astype(BF16)
        gate_pre = _dot(g_low, wg2_ref[...])
        q = proj[rows, OFF_GQ:OFF_GQ + LANES] * (GLA_KEY_DIM ** -0.5)
        k = proj[rows, OFF_GK:OFF_GK + LANES]
        v = proj[rows, OFF_GV:OFF_GV + GROUP_WIDTH]
        out_gate = proj[rows, OFF_GR:OFF_GR + GROUP_WIDTH]
        yield
        log_a = -_softplus(-(gate_pre + bg_ref[...])) * (1.0 / GLA_GATE_NORM)
        cum = _split_dot(tril_gla, log_a)
        yield
        first = row128 < GLA_CHUNK
        cum_end = jnp.where(first, cum[GLA_CHUNK - 1:GLA_CHUNK, :], cum[CHUNK - 1:CHUNK, :])
        q_in = (q * jnp.exp(cum)).astype(BF16)
        k_in = k * jnp.exp(-cum)
        k_end_t = (k * jnp.exp(cum_end - cum)).T.astype(BF16)
        dec_t = jnp.exp(cum_end).T
        dec_sw = pltpu.roll(dec_t, GLA_CHUNK, axis=1)
        dec_by_chunk = (jnp.where(low_half, dec_t, dec_sw), jnp.where(low_half, dec_sw, dec_t))
        n_sub = CHUNK // GLA_CHUNK
        scores, new_states = [], []
        for cc in range(n_sub):
            rr = slice(cc * GLA_CHUNK, (cc + 1) * GLA_CHUNK)
            k_rep = jnp.concatenate([k_in[rr, :]] * N_HEADS, axis=0)
            k_bd = jnp.where(gla_k_diag, k_rep, 0.0).astype(BF16)
            scores.append(_dot_nt(q_in[rr, :], k_bd))
            v_only = jnp.where(gla_row_chunk == cc, v, 0.0).astype(BF16)
            new_states.append(_dot(k_end_t, v_only))
        yield
        y_intra, y_inter = [], []
        for cc in range(n_sub):
            rr = slice(cc * GLA_CHUNK, (cc + 1) * GLA_CHUNK)
            masked = jnp.where(gla_causal, scores[cc], 0.0).astype(BF16)
            v_rep = jnp.concatenate([v[rr, :]] * N_HEADS, axis=0)
            v_bd = jnp.where(gla_v_diag, v_rep, 0.0).astype(BF16)
            y_intra.append(_dot(masked, v_bd))
            prev = gla_st[...]
            y_inter.append(_dot(q_in[rr, :], prev.astype(BF16)))
            dec = jnp.concatenate([dec_by_chunk[cc]] * 2, axis=1)
            gla_st[...] = prev * dec + jnp.where(gla_st_diag, new_states[cc], 0.0)
        yield
        y = jnp.concatenate([y_intra[cc] + y_inter[cc] for cc in range(n_sub)], axis=0)
        ms = _dot((y * y).astype(BF16), head_avg)
        yield
        y = y * lax.rsqrt(ms + EPS) * gla_nw_ref[...]
        y_gla[rows, :] = (y * _silu(out_gate)).astype(BF16)

    h_next = modulated(x_next_ref, mod_next_ref)
    tasks = [task(c) for task in (ssd_task, ret_task, lru_task, gla_task) for c in range(n_chunks)]
    fillers = all_pieces(h_next)
    n_rounds = 6
    per_round = -(-len(fillers) // (n_rounds - 1))
    while tasks:
        alive = []
        for task in tasks:
            try:
                next(task)
                alive.append(task)
            except StopIteration:
                pass
        tasks = alive
        for filler in fillers[:per_round]:
            filler()
        fillers = fillers[per_round:]
    for filler in fillers:
        filler()

    x = x_ref[0]
    gate = mod_ref[0, 2:3, :]
    y_refs = (y_ssd, y_ret, y_lru, y_gla)
    out_parts = [_dot(y_refs[g][...], w_out_ref[g * GROUP_WIDTH:(g + 1) * GROUP_WIDTH, :])
                 for g in range(len(y_refs))]

    keep_tail = jnp.where(step == n_steps - 1, 0.0, 1.0)
    xbc_ext[0:SUBLANES, :] = xbc_ext[0:SUBLANES, :] * keep_tail
    lx_ext[0:SUBLANES, :] = lx_ext[0:SUBLANES, :] * keep_tail
    convolve()
    mixed = out_parts[0] + out_parts[1] + out_parts[2] + out_parts[3]
    o_ref[0] = _layer_norm(DEEPNORM_ALPHA * x + gate * mixed, ln_w_ref[...], ln_b_ref[...])


class _OfLayer(NamedTuple):
    stacked: jax.Array
    layer: int


def _array_of(param):
    return param.stacked if isinstance(param, _OfLayer) else param


def _resident_spec(param):
    if isinstance(param, _OfLayer):
        shape = param.stacked.shape
        return pl.BlockSpec((None,) + shape[1:],
                            lambda *_, _l=param.layer, _n=len(shape): (_l,) + (0,) * (_n - 1))
    return pl.BlockSpec(param.shape, lambda *_, _n=param.ndim: (0,) * _n)


def _mixer_call(x, mod, cos, sin, params):
    bsz, seq, d = x.shape
    tc = MIX_TOKENS
    n_steps = seq // tc
    tok_spec = lambda w: pl.BlockSpec((1, tc, w), lambda b, s: (b, s, 0))

    def next_batch(b, s):
        return jnp.minimum(b + (s + 1) // n_steps, bsz - 1)

    in_specs = [tok_spec(d),
                pl.BlockSpec((1, 3, d), lambda b, s: (b, 0, 0)),
                pl.BlockSpec((1, tc, d), lambda b, s: (next_batch(b, s), (s + 1) % n_steps, 0)),
                pl.BlockSpec((1, 3, d), lambda b, s: (next_batch(b, s), 0, 0)),
                tok_spec(LANES), tok_spec(LANES)] + [_resident_spec(p) for p in params]
    scratch = [
        pltpu.VMEM((tc, OFF_RQ - OFF_Z), F32),
        pltpu.VMEM((tc, OFF_LG - OFF_RQ), F32),
        pltpu.VMEM((tc, OFF_GQ - OFF_LG), F32),
        pltpu.VMEM((tc, PROJ_W - OFF_GQ), F32),
        pltpu.VMEM((tc, GROUP_WIDTH), BF16),
        pltpu.VMEM((tc, GROUP_WIDTH), BF16),
        pltpu.VMEM((tc, GROUP_WIDTH), BF16),
        pltpu.VMEM((tc, GROUP_WIDTH), BF16),
        pltpu.VMEM((tc + SUBLANES, SSD_XBC), F32),
        pltpu.VMEM((tc + SUBLANES, GROUP_WIDTH), F32),
        pltpu.VMEM((2, SSD_STATE, LANES), F32),
        pltpu.VMEM((2, LANES, LANES), F32),
        pltpu.VMEM((LANES, GROUP_WIDTH), F32),
        pltpu.VMEM((SUBLANES, GROUP_WIDTH), F32),
        pltpu.VMEM((2, CHUNK, GROUP_WIDTH), F32),
        pltpu.VMEM((CHUNK, GROUP_WIDTH), F32),
        pltpu.VMEM((CHUNK, GROUP_WIDTH), F32),
        pltpu.VMEM((SUBLANES, GROUP_WIDTH), F32),
    ]
    return pl.pallas_call(
        _mixer_kernel,
        grid=(bsz, seq // tc),
        in_specs=in_specs,
        out_specs=pl.BlockSpec((1, tc, d), lambda b, s: (b, s, 0)),
        out_shape=jax.ShapeDtypeStruct((bsz, seq, d), F32),
        scratch_shapes=scratch,
        compiler_params=pltpu.CompilerParams(
            dimension_semantics=("arbitrary", "arbitrary"), vmem_limit_bytes=VMEM_LIMIT),
        name="token_mixer",
    )(x, mod, x, mod, cos, sin, *[_array_of(p) for p in params])


def _ffn_kernel(x_ref, mod_ref, x_prev_ref, mod_prev_ref, w_up_ref, w_down_ref, ln_w_ref, ln_b_ref,
                o_ref, acc_ref, pre_ln_ref):
    step = pl.program_id(0)
    tm = x_ref.shape[1]
    n_inner = D_FF // FFN_BLOCK
    strip = tm // FFN_LN_STRIPS

    @pl.when(step == 0)
    def _nothing_to_normalise_yet():
        pre_ln_ref[...] = jnp.zeros_like(pre_ln_ref)

    runtime_zero = (step >> 30).astype(F32)
    gate_prev = mod_prev_ref[0, 2:3, :]
    h_in = (x_ref[0] * (1.0 + mod_ref[0, 1:2, :]) + mod_ref[0, 0:1, :]).astype(BF16)
    for j in range(n_inner):
        cols = slice(j * FFN_BLOCK, (j + 1) * FFN_BLOCK)
        g = _dot(h_in, w_up_ref[:, cols])
        u = _dot(h_in, w_up_ref[:, D_FF + j * FFN_BLOCK:D_FF + (j + 1) * FFN_BLOCK])
        if j < FFN_LN_STRIPS:
            rows = pl.ds(j * strip, strip)
            normed = _layer_norm(DEEPNORM_ALPHA * x_prev_ref[0, rows, :] + gate_prev * pre_ln_ref[rows, :],
                                 ln_w_ref[...], ln_b_ref[...])
            o_ref[0, rows, :] = normed
            folded = normed.reshape(strip // SUBLANES, SUBLANES, normed.shape[1]).sum(axis=0)
            tie = runtime_zero * sum(folded[:, k * FFN_BLOCK:(k + 1) * FFN_BLOCK]
                                     for k in range(folded.shape[1] // FFN_BLOCK))
            u = jnp.concatenate([u[0:SUBLANES, :] + tie, u[SUBLANES:, :]], axis=0)
        act = (_silu(g) * u).astype(BF16)
        part = _dot(act, w_down_ref[cols, :])
        if j == 0:
            acc_ref[...] = part
        elif j < n_inner - 1:
            acc_ref[...] += part
        else:
            pre_ln_ref[...] = acc_ref[...] + part


def _ffn_call(x, mod, w_up, w_down, ln_w, ln_b):
    bsz, seq, d = x.shape
    tm = FFN_TOKENS
    per_seq = seq // tm
    n_blocks = bsz * per_seq

    def cur(i):
        blk = jnp.minimum(i, n_blocks - 1)
        return blk // per_seq, blk % per_seq

    def prev(i):
        blk = jnp.maximum(i - 1, 0)
        return blk // per_seq, blk % per_seq

    return pl.pallas_call(
        _ffn_kernel,
        grid=(n_blocks + 1,),
        in_specs=[pl.BlockSpec((1, tm, d), lambda i: (*cur(i), 0)),
                  pl.BlockSpec((1, 3, d), lambda i: (cur(i)[0], 0, 0)),
                  pl.BlockSpec((1, tm, d), lambda i: (*prev(i), 0)),
                  pl.BlockSpec((1, 3, d), lambda i: (prev(i)[0], 0, 0)),
                  _resident_spec(w_up), _resident_spec(w_down), _resident_spec(ln_w), _resident_spec(ln_b)],
        out_specs=pl.BlockSpec((1, tm, d), lambda i: (*prev(i), 0)),
        out_shape=jax.ShapeDtypeStruct((bsz, seq, d), F32),
        scratch_shapes=[pltpu.VMEM((tm, d), F32), pltpu.VMEM((tm, d), F32)],
        compiler_params=pltpu.CompilerParams(
            dimension_semantics=("arbitrary",), vmem_limit_bytes=VMEM_LIMIT),
        name="swiglu_ffn",
    )(x, mod, x, mod, _array_of(w_up), _array_of(w_down), ln_w, ln_b)


def _relayout_w_in(w_in):
    starts = np.concatenate([[0], np.cumsum(IN_SPLITS)]).tolist()
    (z, xbc, dt, rq, rk, rv, rg, lg, lx, gq, gk, gv, glow, gr) = [
        w_in[..., starts[i]:starts[i + 1]] for i in range(len(IN_SPLITS))]
    dt_rep = jnp.repeat(dt, HEAD_DIM, axis=-1)
    glow_pad = jnp.pad(glow, ((0, 0), (0, 0), (0, LANES - GLA_GATE_RANK)))
    out = jnp.concatenate([z, xbc, dt_rep, rq, rk, rv, rg, lg, lx, gq, gk, gv, gr, glow_pad], axis=-1)
    assert out.shape[-1] == PROJ_W
    return out.astype(BF16)


def _block_diag4(w):
    out = jnp.zeros((GROUP_WIDTH, GROUP_WIDTH), w.dtype)
    for i in range(w.shape[0]):
        out = out.at[i * HEAD_DIM:(i + 1) * HEAD_DIM, i * HEAD_DIM:(i + 1) * HEAD_DIM].set(w[i])
    return out


def _row(v):
    return v.reshape(1, -1).astype(F32)


def _rep_heads(v):
    return jnp.repeat(v.astype(F32), HEAD_DIM).reshape(1, GROUP_WIDTH)


def kernel(x, c, positions, w_ada, b_ada, w_in, ssd_conv_w, ssd_conv_b, ssd_dt_bias, ssd_a_log, ssd_d,
           ssd_norm_w, ret_norm_w, lru_conv_w, lru_conv_b, lru_wa, lru_ba, lru_wx, lru_bx, lru_lambda,
           gla_wg2, gla_bg, gla_norm_w, w_out, ln1_w, ln1_b, ffn_w_up, ffn_w_down, ln2_w, ln2_b):
    bsz = x.shape[0]
    depth = w_in.shape[0]
    mod = _ada_call(c, w_ada, b_ada).reshape(depth, bsz, 6, D_MODEL)
    cos, sin = _rope_call(positions)

    w_in_all = _relayout_w_in(w_in)
    w_out_all = w_out.astype(BF16)
    w_up_all = ffn_w_up.astype(BF16)
    w_down_all = ffn_w_down.astype(BF16)

    for l in range(depth):
        wg2_p = jnp.zeros((LANES, LANES), F32).at[0:GLA_GATE_RANK, :].set(gla_wg2[l]).astype(BF16)
        lru_w = jnp.concatenate([_block_diag4(lru_wa[l]), _block_diag4(lru_wx[l])], axis=1).astype(BF16)
        params = (
            _OfLayer(w_in_all, l), _OfLayer(w_out_all, l),
            ssd_conv_w[l], _row(ssd_conv_b[l]), _rep_heads(ssd_dt_bias[l]), _rep_heads(ssd_a_log[l]),
            _rep_heads(ssd_d[l]), _row(ssd_norm_w[l]), _row(ret_norm_w[l]),
            lru_conv_w[l], _row(lru_conv_b[l]), lru_w,
            _row(jnp.concatenate([lru_ba[l], lru_bx[l]])), _row(lru_lambda[l]),
            wg2_p, _row(gla_bg[l]), _row(gla_norm_w[l]), _row(ln1_w[l]), _row(ln1_b[l]),
        )
        x = _mixer_call(x, mod[l, :, 0:3, :], cos, sin, params)
        x = _ffn_call(x, mod[l, :, 3:6, :], _OfLayer(w_up_all, l), _OfLayer(w_down_all, l),
                      _row(ln2_w[l]), _row(ln2_b[l]))
    return x
```

```python
import functools
import math
from typing import NamedTuple

import numpy as np
import jax
import jax.numpy as jnp
from jax import lax
from jax.experimental import pallas as pl
from jax.experimental.pallas import tpu as pltpu

F32 = jnp.float32
BF16 = jnp.bfloat16

D_MODEL = 1024
GROUP_WIDTH = 256
HEAD_DIM = 64
N_HEADS = 4
SSD_STATE = 128
SSD_XBC = 768
CONV_K = 4
CHUNK = 128
GLA_CHUNK = 64
GLA_KEY_DIM = 32
GLA_GATE_RANK = 16
GLA_GATE_NORM = 16.0
ROPE_BASE = 10000.0
LRU_C = 8.0
D_FF = 2816
DEPTH = 2
DEEPNORM_ALPHA = (2 * DEPTH) ** 0.25
EPS = 1e-5
IN_SPLITS = (256, 768, 4, 256, 256, 256, 256, 256, 256, 128, 128, 256, 16, 256)

LANES = 128
SUBLANES = 8

OFF_Z = 0
OFF_XBC = 256
OFF_DT = 1024
OFF_RQ = 1280
OFF_RK = 1536
OFF_RV = 1792
OFF_RG = 2048
OFF_LG = 2304
OFF_LX = 2560
OFF_GQ = 2816
OFF_GK = 2944
OFF_GV = 3072
OFF_GR = 3328
OFF_GLOW = 3584
PROJ_W = 3712

MIX_TOKENS = 512
FFN_TOKENS = 512
FFN_BLOCK = 256
ROPE_ROWS_PER_ITER = 4
FFN_LN_STRIPS = 8
VMEM_LIMIT = 56 * 1024 * 1024


def _dot(a, b):
    return jnp.dot(a, b, preferred_element_type=F32)


def _dot_nt(a, b):
    return lax.dot_general(a, b, (((1,), (1,)), ((), ())), preferred_element_type=F32)


def _sigmoid(x):
    return 1.0 / (1.0 + jnp.exp(-x))


def _silu(x):
    return x * _sigmoid(x)


def _softplus(x):
    return jnp.maximum(x, 0.0) + jnp.log(1.0 + jnp.exp(-jnp.abs(x)))


def _gelu_tanh(x):
    return 0.5 * x * (1.0 + jnp.tanh(math.sqrt(2.0 / math.pi) * (x + 0.044715 * (x * x * x))))


def _layer_norm(v, w, b):
    mu = jnp.mean(v, axis=-1, keepdims=True)
    vc = v - mu
    var = jnp.mean(vc * vc, axis=-1, keepdims=True)
    return vc * lax.rsqrt(var + EPS) * w + b


def _split_dot(m_bf16, v):
    hi = v.astype(BF16)
    lo = (v - hi.astype(F32)).astype(BF16)
    return _dot(m_bf16, hi) + _dot(m_bf16, lo)


def _sqrt_nonneg(v):
    return jnp.where(v > 0.0, v * lax.rsqrt(v), 0.0)


def _iota(shape, axis):
    return lax.broadcasted_iota(jnp.int32, shape, axis)


def _block_of(shape, axis, size):
    return _iota(shape, axis) >> (size.bit_length() - 1)


def _within(shape, axis, size):
    return _iota(shape, axis) & (size - 1)


def _ada_kernel(c_ref, w_ref, b_ref, o_ref):
    c_act = _silu(c_ref[...])
    o_ref[0] = jnp.dot(c_act, w_ref[0], preferred_element_type=F32,
                       precision=lax.Precision.HIGHEST) + b_ref[0]


def _ada_call(c, w_ada, b_ada):
    depth, d, n = w_ada.shape
    bsz = c.shape[0]
    nb = n // d
    return pl.pallas_call(
        _ada_kernel,
        grid=(depth, nb),
        in_specs=[pl.BlockSpec((bsz, d), lambda l, j: (0, 0)),
                  pl.BlockSpec((1, d, d), lambda l, j: (l, 0, j)),
                  pl.BlockSpec((1, 1, d), lambda l, j: (l, 0, j))],
        out_specs=pl.BlockSpec((1, bsz, d), lambda l, j: (l, 0, j)),
        out_shape=jax.ShapeDtypeStruct((depth, bsz, n), F32),
        compiler_params=pltpu.CompilerParams(
            dimension_semantics=("arbitrary", "arbitrary"), vmem_limit_bytes=VMEM_LIMIT),
        name="adaln_mod",
    )(c, w_ada, b_ada.reshape(depth, 1, n))


def _rope_kernel(pos_ref, freq_ref, sign_ref, cos_ref, sin_ref):
    n_rows = pos_ref.shape[1]

    half = HEAD_DIM // 2
    n_groups = LANES // half
    lane_group = _block_of((half, LANES), 1, half)

    def by_lane_group(tiles):
        out = tiles[n_groups - 1]
        for g in range(n_groups - 2, -1, -1):
            out = jnp.where(lane_group == g, tiles[g], out)
        return out

    def body(it, carry):
        packed = []
        for c in range(ROPE_ROWS_PER_ITER):
            row = pos_ref[0, pl.ds(it * ROPE_ROWS_PER_ITER + c, 1), :].astype(F32)
            col = jnp.broadcast_to(row, (LANES, LANES)).T
            packed.append(by_lane_group([col[q * half:(q + 1) * half, :] for q in range(n_groups)]))
        ang = jnp.concatenate(packed, axis=0) * freq_ref[...]
        for tables, out_ref, scale in ((jnp.cos(ang), cos_ref, None), (jnp.sin(ang), sin_ref, sign_ref[...])):
            for c in range(ROPE_ROWS_PER_ITER):
                table = tables[c * half:(c + 1) * half, :]
                start = pl.multiple_of((it * ROPE_ROWS_PER_ITER + c) * LANES, LANES)
                shifted = [table] + [pltpu.roll(table, k * half, axis=1) for k in range(1, n_groups)]
                for q in range(n_groups):
                    full = by_lane_group([shifted[(g - q) % n_groups] for g in range(n_groups)])
                    if scale is not None:
                        full = full * scale
                    out_ref[0, pl.ds(start + q * half, half), :] = full
        return carry

    lax.fori_loop(0, n_rows // ROPE_ROWS_PER_ITER, body, 0)


def _rope_call(positions):
    bsz, seq = positions.shape
    half = HEAD_DIM // 2
    inv_freq = ROPE_BASE ** (-jnp.arange(half, dtype=F32) / half)
    lane = np.arange(LANES)
    freq = inv_freq[lane % half].reshape(1, LANES)
    sign = jnp.asarray(np.where((lane % HEAD_DIM) < half, -1.0, 1.0), F32).reshape(1, LANES)
    pos3 = positions.reshape(bsz, seq // LANES, LANES)
    out = jax.ShapeDtypeStruct((bsz, seq, LANES), F32)
    return pl.pallas_call(
        _rope_kernel,
        grid=(bsz,),
        in_specs=[pl.BlockSpec((1, seq // LANES, LANES), lambda b: (b, 0, 0)),
                  pl.BlockSpec((1, LANES), lambda b: (0, 0)),
                  pl.BlockSpec((1, LANES), lambda b: (0, 0))],
        out_specs=[pl.BlockSpec((1, seq, LANES), lambda b: (b, 0, 0)),
                   pl.BlockSpec((1, seq, LANES), lambda b: (b, 0, 0))],
        out_shape=[out, out],
        compiler_params=pltpu.CompilerParams(
            dimension_semantics=("arbitrary",), vmem_limit_bytes=VMEM_LIMIT),
        name="rope_table",
    )(pos3, freq, sign)


def _swap_halves(v):
    width = v.shape[1]
    fwd = pltpu.roll(v, HEAD_DIM // 2, axis=1)
    bwd = pltpu.roll(v, width - HEAD_DIM // 2, axis=1)
    first_half = _within(v.shape, 1, HEAD_DIM) < HEAD_DIM // 2
    return jnp.where(first_half, bwd, fwd)


def _causal_conv(ext_ref, row0, n_rows, w_ref, b_ref):
    first = row0 + SUBLANES - CONV_K + 1
    acc = b_ref[...] + w_ref[0:1, :] * ext_ref[pl.ds(first, n_rows), :]
    for k in range(1, CONV_K):
        acc = acc + w_ref[k:k + 1, :] * ext_ref[pl.ds(first + k, n_rows), :]
    return acc


GROUP_BASES = (OFF_Z, OFF_RQ, OFF_LG, OFF_GQ, PROJ_W)


class _GroupedColumns:
    def __init__(self, refs):
        self.refs = refs

    def _locate(self, cols):
        for g, ref in enumerate(self.refs):
            if GROUP_BASES[g] <= cols.start and cols.stop <= GROUP_BASES[g + 1]:
                return ref, slice(cols.start - GROUP_BASES[g], cols.stop - GROUP_BASES[g])
        raise ValueError(f"columns {cols} straddle head groups")

    def __getitem__(self, idx):
        ref, cols = self._locate(idx[1])
        return ref[idx[0], cols]

    def __setitem__(self, idx, value):
        ref, cols = self._locate(idx[1])
        ref[idx[0], cols] = value


def _mixer_kernel(x_ref, mod_ref, x_next_ref, mod_next_ref, cos_ref, sin_ref, w_in_ref, w_out_ref,
                  ssd_cw_ref, ssd_cb_ref, dtb_ref, alog_ref, dskip_ref, ssd_nw_ref, ret_nw_ref,
                  lru_cw_ref, lru_cb_ref, lru_w_ref, lru_b_ref, lam_ref,
                  wg2_ref, bg_ref, gla_nw_ref, ln_w_ref, ln_b_ref,
                  o_ref,
                  proj_ssd, proj_ret, proj_lru, proj_gla, y_ssd, y_ret, y_lru, y_gla,
                  xbc_ext, lx_ext, ssd_st, ret_st, gla_st, lru_st,
                  ret_l, ret_in, ret_end, ret_dec):
    proj = _GroupedColumns((proj_ssd, proj_ret, proj_lru, proj_gla))
    tc = x_ref.shape[1]
    n_chunks = tc // CHUNK
    step = pl.program_id(1)
    n_steps = pl.num_programs(1)
    body_rows = pl.ds(SUBLANES, tc)

    lane128 = _iota((CHUNK, LANES), 1)
    row128 = _iota((CHUNK, LANES), 0)
    low_half = lane128 < HEAD_DIM
    causal = row128 >= lane128
    tril = jnp.where(causal, 1.0, 0.0).astype(BF16)
    sq = (CHUNK, LANES)
    same_gla_chunk = _block_of(sq, 0, GLA_CHUNK) == _block_of(sq, 1, GLA_CHUNK)
    tril_gla = jnp.where(causal, jnp.where(same_gla_chunk, 1.0, 0.0), 0.0).astype(BF16)
    pair_diag = _block_of(sq, 0, HEAD_DIM) == _block_of(sq, 1, HEAD_DIM)

    head_of_lane = _block_of((1, GROUP_WIDTH), 1, HEAD_DIM)
    log_gamma = jnp.zeros((1, GROUP_WIDTH), F32)
    for h in range(N_HEADS):
        log_gamma = jnp.where(head_of_lane == h, math.log1p(-(2.0 ** (-5 - h))), log_gamma)

    def modulated(xr, mr):
        return (xr[0] * (1.0 + mr[0, 1:2, :]) + mr[0, 0:1, :]).astype(BF16)

    def project(h, first, last):
        value = _dot(h, w_in_ref[:, first:last])
        if OFF_XBC <= first and last <= OFF_DT:
            xbc_ext[body_rows, first - OFF_XBC:last - OFF_XBC] = value
        elif OFF_LX <= first and last <= OFF_GQ:
            lx_ext[body_rows, first - OFF_LX:last - OFF_LX] = value
        else:
            proj[:, first:last] = value

    def convolve():
        proj[:, OFF_XBC:OFF_XBC + SSD_XBC] = _silu(_causal_conv(xbc_ext, 0, tc, ssd_cw_ref, ssd_cb_ref))
        xbc_ext[0:SUBLANES, :] = xbc_ext[pl.ds(tc, SUBLANES), :]
        proj[:, OFF_LX:OFF_LX + GROUP_WIDTH] = _causal_conv(lx_ext, 0, tc, lru_cw_ref, lru_cb_ref)
        lx_ext[0:SUBLANES, :] = lx_ext[pl.ds(tc, SUBLANES), :]

    def clear_conv_tails():
        xbc_ext[0:SUBLANES, :] = jnp.zeros((SUBLANES, SSD_XBC), F32)
        lx_ext[0:SUBLANES, :] = jnp.zeros((SUBLANES, GROUP_WIDTH), F32)

    def pieces(h, first, last):
        return [functools.partial(project, h, lo, min(lo + GROUP_WIDTH, last))
                for lo in range(first, last, GROUP_WIDTH)]

    def all_pieces(h):
        return [p for g in range(len(GROUP_BASES) - 1) for p in pieces(h, GROUP_BASES[g], GROUP_BASES[g + 1])]

    @pl.when((pl.program_id(0) == 0) & (step == 0))
    def _first_block():
        clear_conv_tails()
        h_first = modulated(x_ref, mod_ref)
        for piece in all_pieces(h_first):
            piece()
        convolve()

    @pl.when(step == 0)
    def _init():
        ssd_st[...] = jnp.zeros_like(ssd_st)
        ret_st[...] = jnp.zeros_like(ret_st)
        gla_st[...] = jnp.zeros_like(gla_st)
        lru_st[...] = jnp.zeros_like(lru_st)
        t_col = _iota((CHUNK, GROUP_WIDTH), 0).astype(F32)
        ret_in[...] = jnp.exp((t_col + 1.0) * log_gamma)
        ret_end[...] = jnp.exp((CHUNK - 1.0 - t_col) * log_gamma)
        ret_dec[...] = jnp.exp(float(CHUNK) * jnp.broadcast_to(log_gamma, (SUBLANES, GROUP_WIDTH)))
        dist = (row128 - lane128).astype(F32)
        for h in range(N_HEADS):
            lg = math.log1p(-(2.0 ** (-5 - h)))
            ret_l[h // 2, :, (h % 2) * LANES:(h % 2 + 1) * LANES] = jnp.where(
                causal, jnp.exp(dist * lg), 0.0)

    gsq = (GROUP_WIDTH, GROUP_WIDTH)
    head_avg = jnp.where(_block_of(gsq, 0, HEAD_DIM) == _block_of(gsq, 1, HEAD_DIM),
                         1.0 / HEAD_DIM, 0.0).astype(BF16)
    gla_k_diag = _block_of((GROUP_WIDTH, LANES), 0, GLA_CHUNK) == _block_of((GROUP_WIDTH, LANES), 1, GLA_KEY_DIM)
    gla_v_diag = _block_of(gsq, 0, GLA_CHUNK) == _block_of(gsq, 1, HEAD_DIM)
    gla_causal = _iota((GLA_CHUNK, GROUP_WIDTH), 0) >= _within((GLA_CHUNK, GROUP_WIDTH), 1, GLA_CHUNK)
    gla_st_diag = (_block_of((LANES, GROUP_WIDTH), 0, GLA_KEY_DIM)
                   == _block_of((LANES, GROUP_WIDTH), 1, HEAD_DIM))
    gla_row_chunk = _block_of((CHUNK, GROUP_WIDTH), 0, GLA_CHUNK)

    def ssd_task(c):
        rows = pl.ds(c * CHUNK, CHUNK)
        dt = _softplus(proj[rows, OFF_DT:OFF_DT + GROUP_WIDTH] + dtb_ref[...])
        log_a = -jnp.exp(alog_ref[...]) * dt
        cum = _split_dot(tril, log_a)
        xs = proj[rows, OFF_XBC:OFF_XBC + GROUP_WIDTH]
        z = proj[rows, OFF_Z:OFF_Z + GROUP_WIDTH]
        xdt = xs * dt
        b_ts, c_gs, scores = [], [], []
        for g in range(2):
            b_g = proj[rows, OFF_XBC + GROUP_WIDTH + g * LANES:OFF_XBC + GROUP_WIDTH + (g + 1) * LANES]
            c_g = proj[rows, OFF_XBC + 2 * GROUP_WIDTH + g * LANES:
                       OFF_XBC + 2 * GROUP_WIDTH + (g + 1) * LANES].astype(BF16)
            b_t = b_g.T.astype(BF16)
            b_ts.append(b_t)
            c_gs.append(c_g)
            scores.append(_dot(c_g, b_t))
        yield
        cum_last = cum[CHUNK - 1:CHUNK, :]
        to_end = jnp.exp(cum_last - cum)
        from_start = jnp.exp(cum)
        y_intra, new_states = [], []
        for g in range(2):
            gl = slice(g * LANES, (g + 1) * LANES)
            cum_g = cum[:, gl]
            cum_sw = pltpu.roll(cum_g, HEAD_DIM, axis=1)
            col_a = jnp.where(low_half, cum_g, cum_sw)
            col_b = jnp.where(low_half, cum_sw, cum_g)
            cum_t = cum_g.T
            l_a = jnp.exp(jnp.where(causal, col_a - cum_t[0:1, :], -jnp.inf))
            l_b = jnp.exp(jnp.where(causal, col_b - cum_t[HEAD_DIM:HEAD_DIM + 1, :], -jnp.inf))
            p = jnp.concatenate([scores[g] * l_a, scores[g] * l_b], axis=1).astype(BF16)
            xdt_g = xdt[:, gl]
            v_bd = jnp.concatenate([jnp.where(low_half, xdt_g, 0.0),
                                    jnp.where(low_half, 0.0, xdt_g)], axis=0).astype(BF16)
            y_intra.append(_dot(p, v_bd))
            new_states.append(_dot(b_ts[g], (xdt_g * to_end[:, gl]).astype(BF16)))
        yield
        y_inter = []
        for g in range(2):
            gl = slice(g * LANES, (g + 1) * LANES)
            prev = ssd_st[g]
            y_inter.append(_dot(c_gs[g], prev.astype(BF16)))
            ssd_st[g] = prev * jnp.exp(cum_last[:, gl]) + new_states[g]
        yield
        y = jnp.concatenate([y_intra[g] + from_start[:, g * LANES:(g + 1) * LANES] * y_inter[g]
                             for g in range(2)], axis=1) + xs * dskip_ref[...]
        y = y * _silu(z)
        y = y * lax.rsqrt(jnp.mean(y * y, axis=-1, keepdims=True) + EPS) * ssd_nw_ref[...]
        y_ssd[rows, :] = y.astype(BF16)

    def ret_task(c):
        rows = pl.ds(c * CHUNK, CHUNK)
        cos = cos_ref[0, rows, :]
        sin = sin_ref[0, rows, :]
        cos2 = jnp.concatenate([cos, cos], axis=1)
        sin2 = jnp.concatenate([sin, sin], axis=1)
        q = proj[rows, OFF_RQ:OFF_RQ + GROUP_WIDTH]
        k = proj[rows, OFF_RK:OFF_RK + GROUP_WIDTH]
        q = q * cos2 + _swap_halves(q) * sin2
        k = (k * cos2 + _swap_halves(k) * sin2) * (HEAD_DIM ** -0.5)
        v = proj[rows, OFF_RV:OFF_RV + GROUP_WIDTH]
        out_gate = proj[rows, OFF_RG:OFF_RG + GROUP_WIDTH]
        v_end = v * ret_end[...]
        q_bf = q.astype(BF16)
        scores, new_states = [], []
        for p_i in range(2):
            pl_ = slice(p_i * LANES, (p_i + 1) * LANES)
            k_t = k[:, pl_].T
            k_bd = jnp.concatenate([jnp.where(row128 < HEAD_DIM, k_t, 0.0),
                                    jnp.where(row128 < HEAD_DIM, 0.0, k_t)], axis=1).astype(BF16)
            scores.append(_dot(q_bf[:, pl_], k_bd))
            new_states.append(_dot(k_t.astype(BF16), v_end[:, pl_].astype(BF16)))
        yield
        y_intra, y_inter = [], []
        for p_i in range(2):
            pl_ = slice(p_i * LANES, (p_i + 1) * LANES)
            p = (scores[p_i] * ret_l[p_i]).astype(BF16)
            v_p = v[:, pl_]
            v_bd = jnp.concatenate([jnp.where(low_half, v_p, 0.0),
                                    jnp.where(low_half, 0.0, v_p)], axis=0).astype(BF16)
            y_intra.append(_dot(p, v_bd))
            prev = ret_st[p_i]
            y_inter.append(_dot(q_bf[:, pl_], prev.astype(BF16)))
            ret_st[p_i] = prev * ret_dec[0:1, pl_] + jnp.where(pair_diag, new_states[p_i], 0.0)
        yield
        y = jnp.concatenate([y_intra[p_i] + ret_in[:, p_i * LANES:(p_i + 1) * LANES] * y_inter[p_i]
                             for p_i in range(2)], axis=1)
        mu = _dot(y.astype(BF16), head_avg)
        yield
        yc = y - mu
        var = _dot((yc * yc).astype(BF16), head_avg)
        yield
        y = yc * lax.rsqrt(var + EPS) * ret_nw_ref[...]
        y_ret[rows, :] = (y * _silu(out_gate)).astype(BF16)

    def lru_task(c):
        rows = pl.ds(c * CHUNK, CHUNK)
        xr = proj[rows, OFF_LX:OFF_LX + GROUP_WIDTH]
        gates = _dot(xr.astype(BF16), lru_w_ref[...])
        out_gate = proj[rows, OFF_LG:OFF_LG + GROUP_WIDTH]
        yield
        gates = gates + lru_b_ref[...]
        r_gate = _sigmoid(gates[:, 0:GROUP_WIDTH])
        i_gate = _sigmoid(gates[:, GROUP_WIDTH:2 * GROUP_WIDTH])
        log_a = -LRU_C * r_gate * _softplus(-lam_ref[...])
        a = jnp.exp(log_a)
        u = _sqrt_nonneg(1.0 - jnp.exp(2.0 * log_a)) * (i_gate * xr)
        n_groups = CHUNK // SUBLANES
        a3 = a.reshape(n_groups, SUBLANES, GROUP_WIDTH)
        u3 = u.reshape(n_groups, SUBLANES, GROUP_WIDTH)
        sub = _iota((1, SUBLANES, GROUP_WIDTH), 1)
        d = 1
        while d < SUBLANES:
            keep = sub >= d
            u3 = jnp.where(keep, a3 * pltpu.roll(u3, d, axis=1) + u3, u3)
            a3 = jnp.where(keep, a3 * pltpu.roll(a3, d, axis=1), a3)
            d *= 2
        h_prev = lru_st[0:1, :]
        gelu_gate = _gelu_tanh(out_gate)
        for grp in range(n_groups):
            h_grp = u3[grp] + a3[grp] * h_prev
            h_prev = h_grp[SUBLANES - 1:SUBLANES, :]
            r0 = c * CHUNK + grp * SUBLANES
            y_lru[r0:r0 + SUBLANES, :] = (
                h_grp * gelu_gate[grp * SUBLANES:(grp + 1) * SUBLANES, :]).astype(BF16)
        lru_st[...] = jnp.broadcast_to(h_prev, lru_st.shape)

    def gla_task(c):
        rows = pl.ds(c * CHUNK, CHUNK)
        g_low = proj[rows, OFF_GLOW:OFF_GLOW + LANES].astype(BF16)
        gate_pre = _dot(g_low, wg2_ref[...])
        q = proj[rows, OFF_GQ:OFF_GQ + LANES] * (GLA_KEY_DIM ** -0.5)
        k = proj[rows, OFF_GK:OFF_GK + LANES]
        v = proj[rows, OFF_GV:OFF_GV + GROUP_WIDTH]
        out_gate = proj[rows, OFF_GR:OFF_GR + GROUP_WIDTH]
        yield
        log_a = -_softplus(-(gate_pre + bg_ref[...])) * (1.0 / GLA_GATE_NORM)
        cum = _split_dot(tril_gla, log_a)
        yield
        first = row128 < GLA_CHUNK
        cum_end = jnp.where(first, cum[GLA_CHUNK - 1:GLA_CHUNK, :], cum[CHUNK - 1:CHUNK, :])
        q_in = (q * jnp.exp(cum)).astype(BF16)
        k_in = k * jnp.exp(-cum)
        k_end_t = (k * jnp.exp(cum_end - cum)).T.astype(BF16)
        dec_t = jnp.exp(cum_end).T
        dec_sw = pltpu.roll(dec_t, GLA_CHUNK, axis=1)
        dec_by_chunk = (jnp.where(low_half, dec_t, dec_sw), jnp.where(low_half, dec_sw, dec_t))
        n_sub = CHUNK // GLA_CHUNK
        scores, new_states = [], []
        for cc in range(n_sub):
            rr = slice(cc * GLA_CHUNK, (cc + 1) * GLA_CHUNK)
            k_rep = jnp.concatenate([k_in[rr, :]] * N_HEADS, axis=0)
            k_bd = jnp.where(gla_k_diag, k_rep, 0.0).astype(BF16)
            scores.append(_dot_nt(q_in[rr, :], k_bd))
            v_only = jnp.where(gla_row_chunk == cc, v, 0.0).astype(BF16)
            new_states.append(_dot(k_end_t, v_only))
        yield
        y_intra, y_inter = [], []
        for cc in range(n_sub):
            rr = slice(cc * GLA_CHUNK, (cc + 1) * GLA_CHUNK)
            masked = jnp.where(gla_causal, scores[cc], 0.0).astype(BF16)
            v_rep = jnp.concatenate([v[rr, :]] * N_HEADS, axis=0)
            v_bd = jnp.where(gla_v_diag, v_rep, 0.0).astype(BF16)
            y_intra.append(_dot(masked, v_bd))
            prev = gla_st[...]
            y_inter.append(_dot(q_in[rr, :], prev.astype(BF16)))
            dec = jnp.concatenate([dec_by_chunk[cc]] * 2, axis=1)
            gla_st[...] = prev * dec + jnp.where(gla_st_diag, new_states[cc], 0.0)
        yield
        y = jnp.concatenate([y_intra[cc] + y_inter[cc] for cc in range(n_sub)], axis=0)
        ms = _dot((y * y).astype(BF16), head_avg)
        yield
        y = y * lax.rsqrt(ms + EPS) * gla_nw_ref[...]
        y_gla[rows, :] = (y * _silu(out_gate)).astype(BF16)

    h_next = modulated(x_next_ref, mod_next_ref)
    tasks = [task(c) for task in (ssd_task, ret_task, lru_task, gla_task) for c in range(n_chunks)]
    fillers = all_pieces(h_next)
    n_rounds = 6
    per_round = -(-len(fillers) // (n_rounds - 1))
    while tasks:
        alive = []
        for task in tasks:
            try:
                next(task)
                alive.append(task)
            except StopIteration:
                pass
        tasks = alive
        for filler in fillers[:per_round]:
            filler()
        fillers = fillers[per_round:]
    for filler in fillers:
        filler()

    x = x_ref[0]
    gate = mod_ref[0, 2:3, :]
    y_refs = (y_ssd, y_ret, y_lru, y_gla)
    out_parts = [_dot(y_refs[g][...], w_out_ref[g * GROUP_WIDTH:(g + 1) * GROUP_WIDTH, :])
                 for g in range(len(y_refs))]

    keep_tail = jnp.where(step == n_steps - 1, 0.0, 1.0)
    xbc_ext[0:SUBLANES, :] = xbc_ext[0:SUBLANES, :] * keep_tail
    lx_ext[0:SUBLANES, :] = lx_ext[0:SUBLANES, :] * keep_tail
    convolve()
    mixed = out_parts[0] + out_parts[1] + out_parts[2] + out_parts[3]
    o_ref[0] = _layer_norm(DEEPNORM_ALPHA * x + gate * mixed, ln_w_ref[...], ln_b_ref[...])


class _OfLayer(NamedTuple):
    stacked: jax.Array
    layer: int


def _array_of(param):
    return param.stacked if isinstance(param, _OfLayer) else param


def _resident_spec(param):
    if isinstance(param, _OfLayer):
        shape = param.stacked.shape
        return pl.BlockSpec((None,) + shape[1:],
                            lambda *_, _l=param.layer, _n=len(shape): (_l,) + (0,) * (_n - 1))
    return pl.BlockSpec(param.shape, lambda *_, _n=param.ndim: (0,) * _n)


def _mixer_call(x, mod, cos, sin, params):
    bsz, seq, d = x.shape
    tc = MIX_TOKENS
    n_steps = seq // tc
    tok_spec = lambda w: pl.BlockSpec((1, tc, w), lambda b, s: (b, s, 0))

    def next_batch(b, s):
        return jnp.minimum(b + (s + 1) // n_steps, bsz - 1)

    in_specs = [tok_spec(d),
                pl.BlockSpec((1, 3, d), lambda b, s: (b, 0, 0)),
                pl.BlockSpec((1, tc, d), lambda b, s: (next_batch(b, s), (s + 1) % n_steps, 0)),
                pl.BlockSpec((1, 3, d), lambda b, s: (next_batch(b, s), 0, 0)),
                tok_spec(LANES), tok_spec(LANES)] + [_resident_spec(p) for p in params]
    scratch = [
        pltpu.VMEM((tc, OFF_RQ - OFF_Z), F32),
        pltpu.VMEM((tc, OFF_LG - OFF_RQ), F32),
        pltpu.VMEM((tc, OFF_GQ - OFF_LG), F32),
        pltpu.VMEM((tc, PROJ_W - OFF_GQ), F32),
        pltpu.VMEM((tc, GROUP_WIDTH), BF16),
        pltpu.VMEM((tc, GROUP_WIDTH), BF16),
        pltpu.VMEM((tc, GROUP_WIDTH), BF16),
        pltpu.VMEM((tc, GROUP_WIDTH), BF16),
        pltpu.VMEM((tc + SUBLANES, SSD_XBC), F32),
        pltpu.VMEM((tc + SUBLANES, GROUP_WIDTH), F32),
        pltpu.VMEM((2, SSD_STATE, LANES), F32),
        pltpu.VMEM((2, LANES, LANES), F32),
        pltpu.VMEM((LANES, GROUP_WIDTH), F32),
        pltpu.VMEM((SUBLANES, GROUP_WIDTH), F32),
        pltpu.VMEM((2, CHUNK, GROUP_WIDTH), F32),
        pltpu.VMEM((CHUNK, GROUP_WIDTH), F32),
        pltpu.VMEM((CHUNK, GROUP_WIDTH), F32),
        pltpu.VMEM((SUBLANES, GROUP_WIDTH), F32),
    ]
    return pl.pallas_call(
        _mixer_kernel,
        grid=(bsz, seq // tc),
        in_specs=in_specs,
        out_specs=pl.BlockSpec((1, tc, d), lambda b, s: (b, s, 0)),
        out_shape=jax.ShapeDtypeStruct((bsz, seq, d), F32),
        scratch_shapes=scratch,
        compiler_params=pltpu.CompilerParams(
            dimension_semantics=("arbitrary", "arbitrary"), vmem_limit_bytes=VMEM_LIMIT),
        name="token_mixer",
    )(x, mod, x, mod, cos, sin, *[_array_of(p) for p in params])


def _ffn_kernel(x_ref, mod_ref, x_prev_ref, mod_prev_ref, w_up_ref, w_down_ref, ln_w_ref, ln_b_ref,
                o_ref, acc_ref, pre_ln_ref):
    step = pl.program_id(0)
    tm = x_ref.shape[1]
    n_inner = D_FF // FFN_BLOCK
    strip = tm // FFN_LN_STRIPS

    @pl.when(step == 0)
    def _nothing_to_normalise_yet():
        pre_ln_ref[...] = jnp.zeros_like(pre_ln_ref)

    runtime_zero = (step >> 30).astype(F32)
    gate_prev = mod_prev_ref[0, 2:3, :]
    h_in = (x_ref[0] * (1.0 + mod_ref[0, 1:2, :]) + mod_ref[0, 0:1, :]).astype(BF16)
    for j in range(n_inner):
        cols = slice(j * FFN_BLOCK, (j + 1) * FFN_BLOCK)
        g = _dot(h_in, w_up_ref[:, cols])
        u = _dot(h_in, w_up_ref[:, D_FF + j * FFN_BLOCK:D_FF + (j + 1) * FFN_BLOCK])
        if j < FFN_LN_STRIPS:
            rows = pl.ds(j * strip, strip)
            normed = _layer_norm(DEEPNORM_ALPHA * x_prev_ref[0, rows, :] + gate_prev * pre_ln_ref[rows, :],
                                 ln_w_ref[...], ln_b_ref[...])
            o_ref[0, rows, :] = normed
            folded = normed.reshape(strip // SUBLANES, SUBLANES, normed.shape[1]).sum(axis=0)
            tie = runtime_zero * sum(folded[:, k * FFN_BLOCK:(k + 1) * FFN_BLOCK]
                                     for k in range(folded.shape[1] // FFN_BLOCK))
            u = jnp.concatenate([u[0:SUBLANES, :] + tie, u[SUBLANES:, :]], axis=0)
        act = (_silu(g) * u).astype(BF16)
        part = _dot(act, w_down_ref[cols, :])
        if j == 0:
            acc_ref[...] = part
        elif j < n_inner - 1:
            acc_ref[...] += part
        else:
            pre_ln_ref[...] = acc_ref[...] + part


def _ffn_call(x, mod, w_up, w_down, ln_w, ln_b):
    bsz, seq, d = x.shape
    tm = FFN_TOKENS
    per_seq = seq // tm
    n_blocks = bsz * per_seq

    def cur(i):
        blk = jnp.minimum(i, n_blocks - 1)
        return blk // per_seq, blk % per_seq

    def prev(i):
        blk = jnp.maximum(i - 1, 0)
        return blk // per_seq, blk % per_seq

    return pl.pallas_call(
        _ffn_kernel,
        grid=(n_blocks + 1,),
        in_specs=[pl.BlockSpec((1, tm, d), lambda i: (*cur(i), 0)),
                  pl.BlockSpec((1, 3, d), lambda i: (cur(i)[0], 0, 0)),
                  pl.BlockSpec((1, tm, d), lambda i: (*prev(i), 0)),
                  pl.BlockSpec((1, 3, d), lambda i: (prev(i)[0], 0, 0)),
                  _resident_spec(w_up), _resident_spec(w_down), _resident_spec(ln_w), _resident_spec(ln_b)],
        out_specs=pl.BlockSpec((1, tm, d), lambda i: (*prev(i), 0)),
        out_shape=jax.ShapeDtypeStruct((bsz, seq, d), F32),
        scratch_shapes=[pltpu.VMEM((tm, d), F32), pltpu.VMEM((tm, d), F32)],
        compiler_params=pltpu.CompilerParams(
            dimension_semantics=("arbitrary",), vmem_limit_bytes=VMEM_LIMIT),
        name="swiglu_ffn",
    )(x, mod, x, mod, _array_of(w_up), _array_of(w_down), ln_w, ln_b)


def _relayout_w_in(w_in):
    starts = np.concatenate([[0], np.cumsum(IN_SPLITS)]).tolist()
    (z, xbc, dt, rq, rk, rv, rg, lg, lx, gq, gk, gv, glow, gr) = [
        w_in[..., starts[i]:starts[i + 1]] for i in range(len(IN_SPLITS))]
    dt_rep = jnp.repeat(dt, HEAD_DIM, axis=-1)
    glow_pad = jnp.pad(glow, ((0, 0), (0, 0), (0, LANES - GLA_GATE_RANK)))
    out = jnp.concatenate([z, xbc, dt_rep, rq, rk, rv, rg, lg, lx, gq, gk, gv, gr, glow_pad], axis=-1)
    assert out.shape[-1] == PROJ_W
    return out.astype(BF16)


def _block_diag4(w):
    out = jnp.zeros((GROUP_WIDTH, GROUP_WIDTH), w.dtype)
    for i in range(w.shape[0]):
        out = out.at[i * HEAD_DIM:(i + 1) * HEAD_DIM, i * HEAD_DIM:(i + 1) * HEAD_DIM].set(w[i])
    return out


def _row(v):
    return v.reshape(1, -1).astype(F32)


def _rep_heads(v):
    return jnp.repeat(v.astype(F32), HEAD_DIM).reshape(1, GROUP_WIDTH)


def kernel(x, c, positions, w_ada, b_ada, w_in, ssd_conv_w, ssd_conv_b, ssd_dt_bias, ssd_a_log, ssd_d,
           ssd_norm_w, ret_norm_w, lru_conv_w, lru_conv_b, lru_wa, lru_ba, lru_wx, lru_bx, lru_lambda,
           gla_wg2, gla_bg, gla_norm_w, w_out, ln1_w, ln1_b, ffn_w_up, ffn_w_down, ln2_w, ln2_b):
    bsz = x.shape[0]
    depth = w_in.shape[0]
    mod = _ada_call(c, w_ada, b_ada).reshape(depth, bsz, 6, D_MODEL)
    cos, sin = _rope_call(positions)

    w_in_all = _relayout_w_in(w_in)
    w_out_all = w_out.astype(BF16)
    w_up_all = ffn_w_up.astype(BF16)
    w_down_all = ffn_w_down.astype(BF16)

    for l in range(depth):
        wg2_p = jnp.zeros((LANES, LANES), F32).at[0:GLA_GATE_RANK, :].set(gla_wg2[l]).astype(BF16)
        lru_w = jnp.concatenate([_block_diag4(lru_wa[l]), _block_diag4(lru_wx[l])], axis=1).astype(BF16)
        params = (
            _OfLayer(w_in_all, l), _OfLayer(w_out_all, l),
            ssd_conv_w[l], _row(ssd_conv_b[l]), _rep_heads(ssd_dt_bias[l]), _rep_heads(ssd_a_log[l]),
            _rep_heads(ssd_d[l]), _row(ssd_norm_w[l]), _row(ret_norm_w[l]),
            lru_conv_w[l], _row(lru_conv_b[l]), lru_w,
            _row(jnp.concatenate([lru_ba[l], lru_bx[l]])), _row(lru_lambda[l]),
            wg2_p, _row(gla_bg[l]), _row(gla_norm_w[l]), _row(ln1_w[l]), _row(ln1_b[l]),
        )
        x = _mixer_call(x, mod[l, :, 0:3, :], cos, sin, params)
        x = _ffn_call(x, mod[l, :, 3:6, :], _OfLayer(w_up_all, l), _OfLayer(w_down_all, l),
                      _row(ln2_w[l]), _row(ln2_b[l]))
    return x
```

```python
import functools
import math
from typing import NamedTuple

import numpy as np
import jax
import jax.numpy as jnp
from jax import lax
from jax.experimental import pallas as pl
from jax.experimental.pallas import tpu as pltpu

F32 = jnp.float32
BF16 = jnp.bfloat16

D_MODEL = 1024
GROUP_WIDTH = 256
HEAD_DIM = 64
N_HEADS = 4
SSD_STATE = 128
SSD_XBC = 768
CONV_K = 4
CHUNK = 128
GLA_CHUNK = 64
GLA_KEY_DIM = 32
GLA_GATE_RANK = 16
GLA_GATE_NORM = 16.0
ROPE_BASE = 10000.0
LRU_C = 8.0
D_FF = 2816
DEPTH = 2
DEEPNORM_ALPHA = (2 * DEPTH) ** 0.25
EPS = 1e-5
IN_SPLITS = (256, 768, 4, 256, 256, 256, 256, 256, 256, 128, 128, 256, 16, 256)

LANES = 128
SUBLANES = 8

OFF_Z = 0
OFF_XBC = 256
OFF_DT = 1024
OFF_RQ = 1280
OFF_RK = 1536
OFF_RV = 1792
OFF_RG = 2048
OFF_LG = 2304
OFF_LX = 2560
OFF_GQ = 2816
OFF_GK = 2944
OFF_GV = 3072
OFF_GR = 3328
OFF_GLOW = 3584
PROJ_W = 3712

MIX_TOKENS = 512
FFN_TOKENS = 512
FFN_BLOCK = 256
ROPE_ROWS_PER_ITER = 4
FFN_LN_STRIPS = 8
VMEM_LIMIT = 56 * 1024 * 1024


def _dot(a, b):
    return jnp.dot(a, b, preferred_element_type=F32)


def _dot_nt(a, b):
    return lax.dot_general(a, b, (((1,), (1,)), ((), ())), preferred_element_type=F32)


def _sigmoid(x):
    return 1.0 / (1.0 + jnp.exp(-x))


def _silu(x):
    return x * _sigmoid(x)


def _softplus(x):
    return jnp.maximum(x, 0.0) + jnp.log(1.0 + jnp.exp(-jnp.abs(x)))


def _gelu_tanh(x):
    return 0.5 * x * (1.0 + jnp.tanh(math.sqrt(2.0 / math.pi) * (x + 0.044715 * (x * x * x))))


def _layer_norm(v, w, b):
    mu = jnp.mean(v, axis=-1, keepdims=True)
    vc = v - mu
    var = jnp.mean(vc * vc, axis=-1, keepdims=True)
    return vc * lax.rsqrt(var + EPS) * w + b


def _split_dot(m_bf16, v):
    hi = v.astype(BF16)
    lo = (v - hi.astype(F32)).astype(BF16)
    return _dot(m_bf16, hi) + _dot(m_bf16, lo)


def _sqrt_nonneg(v):
    return jnp.where(v > 0.0, v * lax.rsqrt(v), 0.0)


def _iota(shape, axis):
    return lax.broadcasted_iota(jnp.int32, shape, axis)


def _block_of(shape, axis, size):
    return _iota(shape, axis) >> (size.bit_length() - 1)


def _within(shape, axis, size):
    return _iota(shape, axis) & (size - 1)


def _ada_kernel(c_ref, w_ref, b_ref, o_ref):
    c_act = _silu(c_ref[...])
    o_ref[0] = jnp.dot(c_act, w_ref[0], preferred_element_type=F32,
                       precision=lax.Precision.HIGHEST) + b_ref[0]


def _ada_call(c, w_ada, b_ada):
    depth, d, n = w_ada.shape
    bsz = c.shape[0]
    nb = n // d
    return pl.pallas_call(
        _ada_kernel,
        grid=(depth, nb),
        in_specs=[pl.BlockSpec((bsz, d), lambda l, j: (0, 0)),
                  pl.BlockSpec((1, d, d), lambda l, j: (l, 0, j)),
                  pl.BlockSpec((1, 1, d), lambda l, j: (l, 0, j))],
        out_specs=pl.BlockSpec((1, bsz, d), lambda l, j: (l, 0, j)),
        out_shape=jax.ShapeDtypeStruct((depth, bsz, n), F32),
        compiler_params=pltpu.CompilerParams(
            dimension_semantics=("arbitrary", "arbitrary"), vmem_limit_bytes=VMEM_LIMIT),
        name="adaln_mod",
    )(c, w_ada, b_ada.reshape(depth, 1, n))


def _rope_kernel(pos_ref, freq_ref, sign_ref, cos_ref, sin_ref):
    n_rows = pos_ref.shape[1]

    half = HEAD_DIM // 2
    n_groups = LANES // half
    lane_group = _block_of((half, LANES), 1, half)

    def by_lane_group(tiles):
        out = tiles[n_groups - 1]
        for g in range(n_groups - 2, -1, -1):
            out = jnp.where(lane_group == g, tiles[g], out)
        return out

    def body(it, carry):
        packed = []
        for c in range(ROPE_ROWS_PER_ITER):
            row = pos_ref[0, pl.ds(it * ROPE_ROWS_PER_ITER + c, 1), :].astype(F32)
            col = jnp.broadcast_to(row, (LANES, LANES)).T
            packed.append(by_lane_group([col[q * half:(q + 1) * half, :] for q in range(n_groups)]))
        ang = jnp.concatenate(packed, axis=0) * freq_ref[...]
        for tables, out_ref, scale in ((jnp.cos(ang), cos_ref, None), (jnp.sin(ang), sin_ref, sign_ref[...])):
            for c in range(ROPE_ROWS_PER_ITER):
                table = tables[c * half:(c + 1) * half, :]
                start = pl.multiple_of((it * ROPE_ROWS_PER_ITER + c) * LANES, LANES)
                shifted = [table] + [pltpu.roll(table, k * half, axis=1) for k in range(1, n_groups)]
                for q in range(n_groups):
                    full = by_lane_group([shifted[(g - q) % n_groups] for g in range(n_groups)])
                    if scale is not None:
                        full = full * scale
                    out_ref[0, pl.ds(start + q * half, half), :] = full
        return carry

    lax.fori_loop(0, n_rows // ROPE_ROWS_PER_ITER, body, 0)


def _rope_call(positions):
    bsz, seq = positions.shape
    half = HEAD_DIM // 2
    inv_freq = ROPE_BASE ** (-jnp.arange(half, dtype=F32) / half)
    lane = np.arange(LANES)
    freq = inv_freq[lane % half].reshape(1, LANES)
    sign = jnp.asarray(np.where((lane % HEAD_DIM) < half, -1.0, 1.0), F32).reshape(1, LANES)
    pos3 = positions.reshape(bsz, seq // LANES, LANES)
    out = jax.ShapeDtypeStruct((bsz, seq, LANES), F32)
    return pl.pallas_call(
        _rope_kernel,
        grid=(bsz,),
        in_specs=[pl.BlockSpec((1, seq // LANES, LANES), lambda b: (b, 0, 0)),
                  pl.BlockSpec((1, LANES), lambda b: (0, 0)),
                  pl.BlockSpec((1, LANES), lambda b: (0, 0))],
        out_specs=[pl.BlockSpec((1, seq, LANES), lambda b: (b, 0, 0)),
                   pl.BlockSpec((1, seq, LANES), lambda b: (b, 0, 0))],
        out_shape=[out, out],
        compiler_params=pltpu.CompilerParams(
            dimension_semantics=("arbitrary",), vmem_limit_bytes=VMEM_LIMIT),
        name="rope_table",
    )(pos3, freq, sign)


def _swap_halves(v):
    width = v.shape[1]
    fwd = pltpu.roll(v, HEAD_DIM // 2, axis=1)
    bwd = pltpu.roll(v, width - HEAD_DIM // 2, axis=1)
    first_half = _within(v.shape, 1, HEAD_DIM) < HEAD_DIM // 2
    return jnp.where(first_half, bwd, fwd)


def _causal_conv(ext_ref, row0, n_rows, w_ref, b_ref):
    x = ext_ref[pl.ds(row0 + SUBLANES, n_rows), :]
    before = ext_ref[pl.ds(row0, SUBLANES), :]
    sub = _iota(before.shape, 0)

    def delayed(v, v_before, d):
        rolled = pltpu.roll(v, d, axis=0)
        head = jnp.where(sub < d, pltpu.roll(v_before, d, axis=0), rolled[0:SUBLANES, :])
        return jnp.concatenate([head, rolled[SUBLANES:, :]], axis=0)

    w0, w1, w2, w3 = (w_ref[k:k + 1, :] for k in range(CONV_K))
    x_d = delayed(x, before, 1)
    older = w1 * x + w0 * x_d
    older_before = w1 * before + w0 * pltpu.roll(before, 1, axis=0)
    return b_ref[...] + (w3 * x + w2 * x_d) + delayed(older, older_before, 2)


GROUP_BASES = (OFF_Z, OFF_RQ, OFF_LG, OFF_GQ, PROJ_W)


class _GroupedColumns:
    def __init__(self, refs):
        self.refs = refs

    def _locate(self, cols):
        for g, ref in enumerate(self.refs):
            if GROUP_BASES[g] <= cols.start and cols.stop <= GROUP_BASES[g + 1]:
                return ref, slice(cols.start - GROUP_BASES[g], cols.stop - GROUP_BASES[g])
        raise ValueError(f"columns {cols} straddle head groups")

    def __getitem__(self, idx):
        ref, cols = self._locate(idx[1])
        return ref[idx[0], cols]

    def __setitem__(self, idx, value):
        ref, cols = self._locate(idx[1])
        ref[idx[0], cols] = value


def _mixer_kernel(x_ref, mod_ref, x_next_ref, mod_next_ref, cos_ref, sin_ref, w_in_ref, w_out_ref,
                  ssd_cw_ref, ssd_cb_ref, dtb_ref, alog_ref, dskip_ref, ssd_nw_ref, ret_nw_ref,
                  lru_cw_ref, lru_cb_ref, lru_w_ref, lru_b_ref, lam_ref,
                  wg2_ref, bg_ref, gla_nw_ref, ln_w_ref, ln_b_ref,
                  o_ref,
                  proj_ssd, proj_ret, proj_lru, proj_gla, y_ssd, y_ret, y_lru, y_gla,
                  xbc_ext, lx_ext, ssd_st, ret_st, gla_st, lru_st,
                  ret_l, ret_in, ret_end, ret_dec):
    proj = _GroupedColumns((proj_ssd, proj_ret, proj_lru, proj_gla))
    tc = x_ref.shape[1]
    n_chunks = tc // CHUNK
    step = pl.program_id(1)
    n_steps = pl.num_programs(1)
    body_rows = pl.ds(SUBLANES, tc)

    lane128 = _iota((CHUNK, LANES), 1)
    row128 = _iota((CHUNK, LANES), 0)
    low_half = lane128 < HEAD_DIM
    causal = row128 >= lane128
    tril = jnp.where(causal, 1.0, 0.0).astype(BF16)
    sq = (CHUNK, LANES)
    same_gla_chunk = _block_of(sq, 0, GLA_CHUNK) == _block_of(sq, 1, GLA_CHUNK)
    tril_gla = jnp.where(causal, jnp.where(same_gla_chunk, 1.0, 0.0), 0.0).astype(BF16)
    pair_diag = _block_of(sq, 0, HEAD_DIM) == _block_of(sq, 1, HEAD_DIM)

    head_of_lane = _block_of((1, GROUP_WIDTH), 1, HEAD_DIM)
    log_gamma = jnp.zeros((1, GROUP_WIDTH), F32)
    for h in range(N_HEADS):
        log_gamma = jnp.where(head_of_lane == h, math.log1p(-(2.0 ** (-5 - h))), log_gamma)

    def modulated(xr, mr):
        return (xr[0] * (1.0 + mr[0, 1:2, :]) + mr[0, 0:1, :]).astype(BF16)

    def project(h, first, last):
        value = _dot(h, w_in_ref[:, first:last])
        if OFF_XBC <= first and last <= OFF_DT:
            xbc_ext[body_rows, first - OFF_XBC:last - OFF_XBC] = value
        elif OFF_LX <= first and last <= OFF_GQ:
            lx_ext[body_rows, first - OFF_LX:last - OFF_LX] = value
        else:
            proj[:, first:last] = value

    def convolve():
        proj[:, OFF_XBC:OFF_XBC + SSD_XBC] = _silu(_causal_conv(xbc_ext, 0, tc, ssd_cw_ref, ssd_cb_ref))
        xbc_ext[0:SUBLANES, :] = xbc_ext[pl.ds(tc, SUBLANES), :]
        proj[:, OFF_LX:OFF_LX + GROUP_WIDTH] = _causal_conv(lx_ext, 0, tc, lru_cw_ref, lru_cb_ref)
        lx_ext[0:SUBLANES, :] = lx_ext[pl.ds(tc, SUBLANES), :]

    def clear_conv_tails():
        xbc_ext[0:SUBLANES, :] = jnp.zeros((SUBLANES, SSD_XBC), F32)
        lx_ext[0:SUBLANES, :] = jnp.zeros((SUBLANES, GROUP_WIDTH), F32)

    def pieces(h, first, last):
        return [functools.partial(project, h, lo, min(lo + GROUP_WIDTH, last))
                for lo in range(first, last, GROUP_WIDTH)]

    def all_pieces(h):
        return [p for g in range(len(GROUP_BASES) - 1) for p in pieces(h, GROUP_BASES[g], GROUP_BASES[g + 1])]

    @pl.when((pl.program_id(0) == 0) & (step == 0))
    def _first_block():
        clear_conv_tails()
        h_first = modulated(x_ref, mod_ref)
        for piece in all_pieces(h_first):
            piece()
        convolve()

    @pl.when(step == 0)
    def _init():
        ssd_st[...] = jnp.zeros_like(ssd_st)
        ret_st[...] = jnp.zeros_like(ret_st)
        gla_st[...] = jnp.zeros_like(gla_st)
        lru_st[...] = jnp.zeros_like(lru_st)
        t_col = _iota((CHUNK, GROUP_WIDTH), 0).astype(F32)
        ret_in[...] = jnp.exp((t_col + 1.0) * log_gamma)
        ret_end[...] = jnp.exp((CHUNK - 1.0 - t_col) * log_gamma)
        ret_dec[...] = jnp.exp(float(CHUNK) * jnp.broadcast_to(log_gamma, (SUBLANES, GROUP_WIDTH)))
        dist = (row128 - lane128).astype(F32)
        for h in range(N_HEADS):
            lg = math.log1p(-(2.0 ** (-5 - h)))
            ret_l[h // 2, :, (h % 2) * LANES:(h % 2 + 1) * LANES] = jnp.where(
                causal, jnp.exp(dist * lg), 0.0)

    gsq = (GROUP_WIDTH, GROUP_WIDTH)
    head_avg = jnp.where(_block_of(gsq, 0, HEAD_DIM) == _block_of(gsq, 1, HEAD_DIM),
                         1.0 / HEAD_DIM, 0.0).astype(BF16)
    gla_k_diag = _block_of((GROUP_WIDTH, LANES), 0, GLA_CHUNK) == _block_of((GROUP_WIDTH, LANES), 1, GLA_KEY_DIM)
    gla_v_diag = _block_of(gsq, 0, GLA_CHUNK) == _block_of(gsq, 1, HEAD_DIM)
    gla_causal = _iota((GLA_CHUNK, GROUP_WIDTH), 0) >= _within((GLA_CHUNK, GROUP_WIDTH), 1, GLA_CHUNK)
    gla_st_diag = (_block_of((LANES, GROUP_WIDTH), 0, GLA_KEY_DIM)
                   == _block_of((LANES, GROUP_WIDTH), 1, HEAD_DIM))
    gla_row_chunk = _block_of((CHUNK, GROUP_WIDTH), 0, GLA_CHUNK)

    def ssd_task(c):
        rows = pl.ds(c * CHUNK, CHUNK)
        dt = _softplus(proj[rows, OFF_DT:OFF_DT + GROUP_WIDTH] + dtb_ref[...])
        log_a = -jnp.exp(alog_ref[...]) * dt
        cum = _split_dot(tril, log_a)
        xs = proj[rows, OFF_XBC:OFF_XBC + GROUP_WIDTH]
        z = proj[rows, OFF_Z:OFF_Z + GROUP_WIDTH]
        xdt = xs * dt
        b_ts, c_gs, scores = [], [], []
        for g in range(2):
            b_g = proj[rows, OFF_XBC + GROUP_WIDTH + g * LANES:OFF_XBC + GROUP_WIDTH + (g + 1) * LANES]
            c_g = proj[rows, OFF_XBC + 2 * GROUP_WIDTH + g * LANES:
                       OFF_XBC + 2 * GROUP_WIDTH + (g + 1) * LANES].astype(BF16)
            b_t = b_g.T.astype(BF16)
            b_ts.append(b_t)
            c_gs.append(c_g)
            scores.append(_dot(c_g, b_t))
        yield
        cum_last = cum[CHUNK - 1:CHUNK, :]
        to_end = jnp.exp(cum_last - cum)
        from_start = jnp.exp(cum)
        y_intra, new_states = [], []
        for g in range(2):
            gl = slice(g * LANES, (g + 1) * LANES)
            cum_g = cum[:, gl]
            cum_sw = pltpu.roll(cum_g, HEAD_DIM, axis=1)
            col_a = jnp.where(low_half, cum_g, cum_sw)
            col_b = jnp.where(low_half, cum_sw, cum_g)
            cum_t = cum_g.T
            l_a = jnp.exp(jnp.where(causal, col_a - cum_t[0:1, :], -jnp.inf))
            l_b = jnp.exp(jnp.where(causal, col_b - cum_t[HEAD_DIM:HEAD_DIM + 1, :], -jnp.inf))
            p = jnp.concatenate([scores[g] * l_a, scores[g] * l_b], axis=1).astype(BF16)
            xdt_g = xdt[:, gl]
            v_bd = jnp.concatenate([jnp.where(low_half, xdt_g, 0.0),
                                    jnp.where(low_half, 0.0, xdt_g)], axis=0).astype(BF16)
            y_intra.append(_dot(p, v_bd))
            new_states.append(_dot(b_ts[g], (xdt_g * to_end[:, gl]).astype(BF16)))
        yield
        y_inter = []
        for g in range(2):
            gl = slice(g * LANES, (g + 1) * LANES)
            prev = ssd_st[g]
            y_inter.append(_dot(c_gs[g], prev.astype(BF16)))
            ssd_st[g] = prev * jnp.exp(cum_last[:, gl]) + new_states[g]
        yield
        y = jnp.concatenate([y_intra[g] + from_start[:, g * LANES:(g + 1) * LANES] * y_inter[g]
                             for g in range(2)], axis=1) + xs * dskip_ref[...]
        y = y * _silu(z)
        y = y * lax.rsqrt(jnp.mean(y * y, axis=-1, keepdims=True) + EPS) * ssd_nw_ref[...]
        y_ssd[rows, :] = y.astype(BF16)

    def ret_task(c):
        rows = pl.ds(c * CHUNK, CHUNK)
        cos = cos_ref[0, rows, :]
        sin = sin_ref[0, rows, :]
        cos2 = jnp.concatenate([cos, cos], axis=1)
        sin2 = jnp.concatenate([sin, sin], axis=1)
        q = proj[rows, OFF_RQ:OFF_RQ + GROUP_WIDTH]
        k = proj[rows, OFF_RK:OFF_RK + GROUP_WIDTH]
        q = q * cos2 + _swap_halves(q) * sin2
        k = (k * cos2 + _swap_halves(k) * sin2) * (HEAD_DIM ** -0.5)
        v = proj[rows, OFF_RV:OFF_RV + GROUP_WIDTH]
        out_gate = proj[rows, OFF_RG:OFF_RG + GROUP_WIDTH]
        v_end = v * ret_end[...]
        q_bf = q.astype(BF16)
        scores, new_states = [], []
        for p_i in range(2):
            pl_ = slice(p_i * LANES, (p_i + 1) * LANES)
            k_t = k[:, pl_].T
            k_bd = jnp.concatenate([jnp.where(row128 < HEAD_DIM, k_t, 0.0),
                                    jnp.where(row128 < HEAD_DIM, 0.0, k_t)], axis=1).astype(BF16)
            scores.append(_dot(q_bf[:, pl_], k_bd))
            new_states.append(_dot(k_t.astype(BF16), v_end[:, pl_].astype(BF16)))
        yield
        y_intra, y_inter = [], []
        for p_i in range(2):
            pl_ = slice(p_i * LANES, (p_i + 1) * LANES)
            p = (scores[p_i] * ret_l[p_i]).astype(BF16)
            v_p = v[:, pl_]
            v_bd = jnp.concatenate([jnp.where(low_half, v_p, 0.0),
                                    jnp.where(low_half, 0.0, v_p)], axis=0).astype(BF16)
            y_intra.append(_dot(p, v_bd))
            prev = ret_st[p_i]
            y_inter.append(_dot(q_bf[:, pl_], prev.astype(BF16)))
            ret_st[p_i] = prev * ret_dec[0:1, pl_] + jnp.where(pair_diag, new_states[p_i], 0.0)
        yield
        y = jnp.concatenate([y_intra[p_i] + ret_in[:, p_i * LANES:(p_i + 1) * LANES] * y_inter[p_i]
                             for p_i in range(2)], axis=1)
        mu = _dot(y.astype(BF16), head_avg)
        yield
        yc = y - mu
        var = _dot((yc * yc).astype(BF16), head_avg)
        yield
        y = yc * lax.rsqrt(var + EPS) * ret_nw_ref[...]
        y_ret[rows, :] = (y * _silu(out_gate)).astype(BF16)

    def lru_task(c):
        rows = pl.ds(c * CHUNK, CHUNK)
        xr = proj[rows, OFF_LX:OFF_LX + GROUP_WIDTH]
        gates = _dot(xr.astype(BF16), lru_w_ref[...])
        out_gate = proj[rows, OFF_LG:OFF_LG + GROUP_WIDTH]
        yield
        gates = gates + lru_b_ref[...]
        r_gate = _sigmoid(gates[:, 0:GROUP_WIDTH])
        i_gate = _sigmoid(gates[:, GROUP_WIDTH:2 * GROUP_WIDTH])
        log_a = -LRU_C * r_gate * _softplus(-lam_ref[...])
        a = jnp.exp(log_a)
        u = _sqrt_nonneg(1.0 - jnp.exp(2.0 * log_a)) * (i_gate * xr)
        n_groups = CHUNK // SUBLANES
        a3 = a.reshape(n_groups, SUBLANES, GROUP_WIDTH)
        u3 = u.reshape(n_groups, SUBLANES, GROUP_WIDTH)
        sub = _iota((1, SUBLANES, GROUP_WIDTH), 1)
        d = 1
        while d < SUBLANES:
            keep = sub >= d
            u3 = jnp.where(keep, a3 * pltpu.roll(u3, d, axis=1) + u3, u3)
            a3 = jnp.where(keep, a3 * pltpu.roll(a3, d, axis=1), a3)
            d *= 2
        h_prev = lru_st[0:1, :]
        gelu_gate = _gelu_tanh(out_gate)
        for grp in range(n_groups):
            h_grp = u3[grp] + a3[grp] * h_prev
            h_prev = h_grp[SUBLANES - 1:SUBLANES, :]
            r0 = c * CHUNK + grp * SUBLANES
            y_lru[r0:r0 + SUBLANES, :] = (
                h_grp * gelu_gate[grp * SUBLANES:(grp + 1) * SUBLANES, :]).astype(BF16)
        lru_st[...] = jnp.broadcast_to(h_prev, lru_st.shape)

    def gla_task(c):
        rows = pl.ds(c * CHUNK, CHUNK)
        g_low = proj[rows, OFF_GLOW:OFF_GLOW + LANES].astype(BF16)
        gate_pre = _dot(g_low, wg2_ref[...])
        q = proj[rows, OFF_GQ:OFF_GQ + LANES] * (GLA_KEY_DIM ** -0.5)
        k = proj[rows, OFF_GK:OFF_GK + LANES]
        v = proj[rows, OFF_GV:OFF_GV + GROUP_WIDTH]
        out_gate = proj[rows, OFF_GR:OFF_GR + GROUP_WIDTH]
        yield
        log_a = -_softplus(-(gate_pre + bg_ref[...])) * (1.0 / GLA_GATE_NORM)
        cum = _split_dot(tril_gla, log_a)
        yield
        first = row128 < GLA_CHUNK
        cum_end = jnp.where(first, cum[GLA_CHUNK - 1:GLA_CHUNK, :], cum[CHUNK - 1:CHUNK, :])
        q_in = (q * jnp.exp(cum)).astype(BF16)
        k_in = k * jnp.exp(-cum)
        k_end_t = (k * jnp.exp(cum_end - cum)).T.astype(BF16)
        dec_t = jnp.exp(cum_end).T
        dec_sw = pltpu.roll(dec_t, GLA_CHUNK, axis=1)
        dec_by_chunk = (jnp.where(low_half, dec_t, dec_sw), jnp.where(low_half, dec_sw, dec_t))
        n_sub = CHUNK // GLA_CHUNK
        scores, new_states = [], []
        for cc in range(n_sub):
            rr = slice(cc * GLA_CHUNK, (cc + 1) * GLA_CHUNK)
            k_rep = jnp.concatenate([k_in[rr, :]] * N_HEADS, axis=0)
            k_bd = jnp.where(gla_k_diag, k_rep, 0.0).astype(BF16)
            scores.append(_dot_nt(q_in[rr, :], k_bd))
            v_only = jnp.where(gla_row_chunk == cc, v, 0.0).astype(BF16)
            new_states.append(_dot(k_end_t, v_only))
        yield
        y_intra, y_inter = [], []
        for cc in range(n_sub):
            rr = slice(cc * GLA_CHUNK, (cc + 1) * GLA_CHUNK)
            masked = jnp.where(gla_causal, scores[cc], 0.0).astype(BF16)
            v_rep = jnp.concatenate([v[rr, :]] * N_HEADS, axis=0)
            v_bd = jnp.where(gla_v_diag, v_rep, 0.0).astype(BF16)
            y_intra.append(_dot(masked, v_bd))
            prev = gla_st[...]
            y_inter.append(_dot(q_in[rr, :], prev.astype(BF16)))
            dec = jnp.concatenate([dec_by_chunk[cc]] * 2, axis=1)
            gla_st[...] = prev * dec + jnp.where(gla_st_diag, new_states[cc], 0.0)
        yield
        y = jnp.concatenate([y_intra[cc] + y_inter[cc] for cc in range(n_sub)], axis=0)
        ms = _dot((y * y).astype(BF16), head_avg)
        yield
        y = y * lax.rsqrt(ms + EPS) * gla_nw_ref[...]
        y_gla[rows, :] = (y * _silu(out_gate)).astype(BF16)

    h_next = modulated(x_next_ref, mod_next_ref)
    tasks = [task(c) for task in (ssd_task, ret_task, lru_task, gla_task) for c in range(n_chunks)]
    fillers = all_pieces(h_next)
    n_rounds = 6
    per_round = -(-len(fillers) // (n_rounds - 1))
    while tasks:
        alive = []
        for task in tasks:
            try:
                next(task)
                alive.append(task)
            except StopIteration:
                pass
        tasks = alive
        for filler in fillers[:per_round]:
            filler()
        fillers = fillers[per_round:]
    for filler in fillers:
        filler()

    x = x_ref[0]
    gate = mod_ref[0, 2:3, :]
    y_all = jnp.concatenate([y_ssd[...], y_ret[...], y_lru[...], y_gla[...]], axis=1)
    mixed = _dot(y_all, w_out_ref[...])

    keep_tail = jnp.where(step == n_steps - 1, 0.0, 1.0)
    xbc_ext[0:SUBLANES, :] = xbc_ext[0:SUBLANES, :] * keep_tail
    lx_ext[0:SUBLANES, :] = lx_ext[0:SUBLANES, :] * keep_tail
    convolve()
    o_ref[0] = _layer_norm(DEEPNORM_ALPHA * x + gate * mixed, ln_w_ref[...], ln_b_ref[...])


class _OfLayer(NamedTuple):
    stacked: jax.Array
    layer: int


def _array_of(param):
    return param.stacked if isinstance(param, _OfLayer) else param


def _resident_spec(param):
    if isinstance(param, _OfLayer):
        shape = param.stacked.shape
        return pl.BlockSpec((None,) + shape[1:],
                            lambda *_, _l=param.layer, _n=len(shape): (_l,) + (0,) * (_n - 1))
    return pl.BlockSpec(param.shape, lambda *_, _n=param.ndim: (0,) * _n)


def _mixer_call(x, mod, cos, sin, params):
    bsz, seq, d = x.shape
    tc = MIX_TOKENS
    n_steps = seq // tc
    tok_spec = lambda w: pl.BlockSpec((1, tc, w), lambda b, s: (b, s, 0))

    def next_batch(b, s):
        return jnp.minimum(b + (s + 1) // n_steps, bsz - 1)

    in_specs = [tok_spec(d),
                pl.BlockSpec((1, 3, d), lambda b, s: (b, 0, 0)),
                pl.BlockSpec((1, tc, d), lambda b, s: (next_batch(b, s), (s + 1) % n_steps, 0)),
                pl.BlockSpec((1, 3, d), lambda b, s: (next_batch(b, s), 0, 0)),
                tok_spec(LANES), tok_spec(LANES)] + [_resident_spec(p) for p in params]
    scratch = [
        pltpu.VMEM((tc, OFF_RQ - OFF_Z), F32),
        pltpu.VMEM((tc, OFF_LG - OFF_RQ), F32),
        pltpu.VMEM((tc, OFF_GQ - OFF_LG), F32),
        pltpu.VMEM((tc, PROJ_W - OFF_GQ), F32),
        pltpu.VMEM((tc, GROUP_WIDTH), BF16),
        pltpu.VMEM((tc, GROUP_WIDTH), BF16),
        pltpu.VMEM((tc, GROUP_WIDTH), BF16),
        pltpu.VMEM((tc, GROUP_WIDTH), BF16),
        pltpu.VMEM((tc + SUBLANES, SSD_XBC), F32),
        pltpu.VMEM((tc + SUBLANES, GROUP_WIDTH), F32),
        pltpu.VMEM((2, SSD_STATE, LANES), F32),
        pltpu.VMEM((2, LANES, LANES), F32),
        pltpu.VMEM((LANES, GROUP_WIDTH), F32),
        pltpu.VMEM((SUBLANES, GROUP_WIDTH), F32),
        pltpu.VMEM((2, CHUNK, GROUP_WIDTH), F32),
        pltpu.VMEM((CHUNK, GROUP_WIDTH), F32),
        pltpu.VMEM((CHUNK, GROUP_WIDTH), F32),
        pltpu.VMEM((SUBLANES, GROUP_WIDTH), F32),
    ]
    return pl.pallas_call(
        _mixer_kernel,
        grid=(bsz, seq // tc),
        in_specs=in_specs,
        out_specs=pl.BlockSpec((1, tc, d), lambda b, s: (b, s, 0)),
        out_shape=jax.ShapeDtypeStruct((bsz, seq, d), F32),
        scratch_shapes=scratch,
        compiler_params=pltpu.CompilerParams(
            dimension_semantics=("arbitrary", "arbitrary"), vmem_limit_bytes=VMEM_LIMIT),
        name="token_mixer",
    )(x, mod, x, mod, cos, sin, *[_array_of(p) for p in params])


def _ffn_kernel(x_ref, mod_ref, x_prev_ref, mod_prev_ref, w_up_ref, w_down_ref, ln_w_ref, ln_b_ref,
                o_ref, acc_ref, pre_ln_ref):
    step = pl.program_id(0)
    tm = x_ref.shape[1]
    n_inner = D_FF // FFN_BLOCK
    strip = tm // FFN_LN_STRIPS

    @pl.when(step == 0)
    def _nothing_to_normalise_yet():
        pre_ln_ref[...] = jnp.zeros_like(pre_ln_ref)

    runtime_zero = (step >> 30).astype(F32)
    gate_prev = mod_prev_ref[0, 2:3, :]
    h_in = (x_ref[0] * (1.0 + mod_ref[0, 1:2, :]) + mod_ref[0, 0:1, :]).astype(BF16)
    for j in range(n_inner):
        cols = slice(j * FFN_BLOCK, (j + 1) * FFN_BLOCK)
        g = _dot(h_in, w_up_ref[:, cols])
        u = _dot(h_in, w_up_ref[:, D_FF + j * FFN_BLOCK:D_FF + (j + 1) * FFN_BLOCK])
        if j < FFN_LN_STRIPS:
            rows = pl.ds(j * strip, strip)
            normed = _layer_norm(DEEPNORM_ALPHA * x_prev_ref[0, rows, :] + gate_prev * pre_ln_ref[rows, :],
                                 ln_w_ref[...], ln_b_ref[...])
            o_ref[0, rows, :] = normed
            folded = normed.reshape(strip // SUBLANES, SUBLANES, normed.shape[1]).sum(axis=0)
            tie = runtime_zero * sum(folded[:, k * FFN_BLOCK:(k + 1) * FFN_BLOCK]
                                     for k in range(folded.shape[1] // FFN_BLOCK))
            u = jnp.concatenate([u[0:SUBLANES, :] + tie, u[SUBLANES:, :]], axis=0)
        act = (_silu(g) * u).astype(BF16)
        part = _dot(act, w_down_ref[cols, :])
        if j == 0:
            acc_ref[...] = part
        elif j < n_inner - 1:
            acc_ref[...] += part
        else:
            pre_ln_ref[...] = acc_ref[...] + part


def _ffn_call(x, mod, w_up, w_down, ln_w, ln_b):
    bsz, seq, d = x.shape
    tm = FFN_TOKENS
    per_seq = seq // tm
    n_blocks = bsz * per_seq

    def cur(i):
        blk = jnp.minimum(i, n_blocks - 1)
        return blk // per_seq, blk % per_seq

    def prev(i):
        blk = jnp.maximum(i - 1, 0)
        return blk // per_seq, blk % per_seq

    return pl.pallas_call(
        _ffn_kernel,
        grid=(n_blocks + 1,),
        in_specs=[pl.BlockSpec((1, tm, d), lambda i: (*cur(i), 0)),
                  pl.BlockSpec((1, 3, d), lambda i: (cur(i)[0], 0, 0)),
                  pl.BlockSpec((1, tm, d), lambda i: (*prev(i), 0)),
                  pl.BlockSpec((1, 3, d), lambda i: (prev(i)[0], 0, 0)),
                  _resident_spec(w_up), _resident_spec(w_down), _resident_spec(ln_w), _resident_spec(ln_b)],
        out_specs=pl.BlockSpec((1, tm, d), lambda i: (*prev(i), 0)),
        out_shape=jax.ShapeDtypeStruct((bsz, seq, d), F32),
        scratch_shapes=[pltpu.VMEM((tm, d), F32), pltpu.VMEM((tm, d), F32)],
        compiler_params=pltpu.CompilerParams(
            dimension_semantics=("arbitrary",), vmem_limit_bytes=VMEM_LIMIT),
        name="swiglu_ffn",
    )(x, mod, x, mod, _array_of(w_up), _array_of(w_down), ln_w, ln_b)


def _relayout_w_in(w_in):
    starts = np.concatenate([[0], np.cumsum(IN_SPLITS)]).tolist()
    (z, xbc, dt, rq, rk, rv, rg, lg, lx, gq, gk, gv, glow, gr) = [
        w_in[..., starts[i]:starts[i + 1]] for i in range(len(IN_SPLITS))]
    dt_rep = jnp.repeat(dt, HEAD_DIM, axis=-1)
    glow_pad = jnp.pad(glow, ((0, 0), (0, 0), (0, LANES - GLA_GATE_RANK)))
    out = jnp.concatenate([z, xbc, dt_rep, rq, rk, rv, rg, lg, lx, gq, gk, gv, gr, glow_pad], axis=-1)
    assert out.shape[-1] == PROJ_W
    return out.astype(BF16)


def _block_diag4(w):
    out = jnp.zeros((GROUP_WIDTH, GROUP_WIDTH), w.dtype)
    for i in range(w.shape[0]):
        out = out.at[i * HEAD_DIM:(i + 1) * HEAD_DIM, i * HEAD_DIM:(i + 1) * HEAD_DIM].set(w[i])
    return out


def _row(v):
    return v.reshape(1, -1).astype(F32)


def _rep_heads(v):
    return jnp.repeat(v.astype(F32), HEAD_DIM).reshape(1, GROUP_WIDTH)


def kernel(x, c, positions, w_ada, b_ada, w_in, ssd_conv_w, ssd_conv_b, ssd_dt_bias, ssd_a_log, ssd_d,
           ssd_norm_w, ret_norm_w, lru_conv_w, lru_conv_b, lru_wa, lru_ba, lru_wx, lru_bx, lru_lambda,
           gla_wg2, gla_bg, gla_norm_w, w_out, ln1_w, ln1_b, ffn_w_up, ffn_w_down, ln2_w, ln2_b):
    bsz = x.shape[0]
    depth = w_in.shape[0]
    mod = _ada_call(c, w_ada, b_ada).reshape(depth, bsz, 6, D_MODEL)
    cos, sin = _rope_call(positions)

    w_in_all = _relayout_w_in(w_in)
    w_out_all = w_out.astype(BF16)
    w_up_all = ffn_w_up.astype(BF16)
    w_down_all = ffn_w_down.astype(BF16)

    for l in range(depth):
        wg2_p = jnp.zeros((LANES, LANES), F32).at[0:GLA_GATE_RANK, :].set(gla_wg2[l]).astype(BF16)
        lru_w = jnp.concatenate([_block_diag4(lru_wa[l]), _block_diag4(lru_wx[l])], axis=1).astype(BF16)
        params = (
            _OfLayer(w_in_all, l), _OfLayer(w_out_all, l),
            ssd_conv_w[l], _row(ssd_conv_b[l]), _rep_heads(ssd_dt_bias[l]), _rep_heads(ssd_a_log[l]),
            _rep_heads(ssd_d[l]), _row(ssd_norm_w[l]), _row(ret_norm_w[l]),
            lru_conv_w[l], _row(lru_conv_b[l]), lru_w,
            _row(jnp.concatenate([lru_ba[l], lru_bx[l]])), _row(lru_lambda[l]),
            wg2_p, _row(gla_bg[l]), _row(gla_norm_w[l]), _row(ln1_w[l]), _row(ln1_b[l]),
        )
        x = _mixer_call(x, mod[l, :, 0:3, :], cos, sin, params)
        x = _ffn_call(x, mod[l, :, 3:6, :], _OfLayer(w_up_all, l), _OfLayer(w_down_all, l),
                      _row(ln2_w[l]), _row(ln2_b[l]))
    return x
```

```python
import functools
import math
from typing import NamedTuple

import numpy as np
import jax
import jax.numpy as jnp
from jax import lax
from jax.experimental import pallas as pl
from jax.experimental.pallas import tpu as pltpu

F32 = jnp.float32
BF16 = jnp.bfloat16

D_MODEL = 1024
GROUP_WIDTH = 256
HEAD_DIM = 64
N_HEADS = 4
SSD_STATE = 128
SSD_XBC = 768
CONV_K = 4
CHUNK = 128
GLA_CHUNK = 64
GLA_KEY_DIM = 32
GLA_GATE_RANK = 16
GLA_GATE_NORM = 16.0
ROPE_BASE = 10000.0
LRU_C = 8.0
D_FF = 2816
DEPTH = 2
DEEPNORM_ALPHA = (2 * DEPTH) ** 0.25
EPS = 1e-5
IN_SPLITS = (256, 768, 4, 256, 256, 256, 256, 256, 256, 128, 128, 256, 16, 256)

LANES = 128
SUBLANES = 8

OFF_Z = 0
OFF_XBC = 256
OFF_DT = 1024
OFF_RQ = 1280
OFF_RK = 1536
OFF_RV = 1792
OFF_RG = 2048
OFF_LG = 2304
OFF_LX = 2560
OFF_GQ = 2816
OFF_GK = 2944
OFF_GV = 3072
OFF_GR = 3328
OFF_GLOW = 3584
PROJ_W = 3712

MIX_TOKENS = 512
FFN_TOKENS = 512
FFN_BLOCK = 256
ROPE_ROWS_PER_ITER = 4
FFN_LN_STRIPS = 8
VMEM_LIMIT = 56 * 1024 * 1024


def _dot(a, b):
    return jnp.dot(a, b, preferred_element_type=F32)


def _dot_nt(a, b):
    return lax.dot_general(a, b, (((1,), (1,)), ((), ())), preferred_element_type=F32)


def _sigmoid(x):
    return 1.0 / (1.0 + jnp.exp(-x))


def _silu(x):
    return x * _sigmoid(x)


def _softplus(x):
    return jnp.maximum(x, 0.0) + jnp.log(1.0 + jnp.exp(-jnp.abs(x)))


def _gelu_tanh(x):
    return 0.5 * x * (1.0 + jnp.tanh(math.sqrt(2.0 / math.pi) * (x + 0.044715 * (x * x * x))))


def _layer_norm(v, w, b):
    mu = jnp.mean(v, axis=-1, keepdims=True)
    vc = v - mu
    var = jnp.mean(vc * vc, axis=-1, keepdims=True)
    return vc * lax.rsqrt(var + EPS) * w + b


def _split_dot(m_bf16, v):
    hi = v.astype(BF16)
    lo = (v - hi.astype(F32)).astype(BF16)
    return _dot(m_bf16, hi) + _dot(m_bf16, lo)


def _sqrt_nonneg(v):
    return jnp.where(v > 0.0, v * lax.rsqrt(v), 0.0)


def _iota(shape, axis):
    return lax.broadcasted_iota(jnp.int32, shape, axis)


def _block_of(shape, axis, size):
    return _iota(shape, axis) >> (size.bit_length() - 1)


def _within(shape, axis, size):
    return _iota(shape, axis) & (size - 1)


def _ada_kernel(c_ref, w_ref, b_ref, o_ref):
    c_act = _silu(c_ref[...])
    o_ref[0] = jnp.dot(c_act, w_ref[0], preferred_element_type=F32,
                       precision=lax.Precision.HIGHEST) + b_ref[0]


def _ada_call(c, w_ada, b_ada):
    depth, d, n = w_ada.shape
    bsz = c.shape[0]
    nb = n // d
    return pl.pallas_call(
        _ada_kernel,
        grid=(depth, nb),
        in_specs=[pl.BlockSpec((bsz, d), lambda l, j: (0, 0)),
                  pl.BlockSpec((1, d, d), lambda l, j: (l, 0, j)),
                  pl.BlockSpec((1, 1, d), lambda l, j: (l, 0, j))],
        out_specs=pl.BlockSpec((1, bsz, d), lambda l, j: (l, 0, j)),
        out_shape=jax.ShapeDtypeStruct((depth, bsz, n), F32),
        compiler_params=pltpu.CompilerParams(
            dimension_semantics=("arbitrary", "arbitrary"), vmem_limit_bytes=VMEM_LIMIT),
        name="adaln_mod",
    )(c, w_ada, b_ada.reshape(depth, 1, n))


def _rope_kernel(pos_ref, freq_ref, sign_ref, cos_ref, sin_ref):
    n_rows = pos_ref.shape[1]

    half = HEAD_DIM // 2
    n_groups = LANES // half
    lane_group = _block_of((half, LANES), 1, half)

    def by_lane_group(tiles):
        out = tiles[n_groups - 1]
        for g in range(n_groups - 2, -1, -1):
            out = jnp.where(lane_group == g, tiles[g], out)
        return out

    def body(it, carry):
        packed = []
        for c in range(ROPE_ROWS_PER_ITER):
            row = pos_ref[0, pl.ds(it * ROPE_ROWS_PER_ITER + c, 1), :].astype(F32)
            col = jnp.broadcast_to(row, (LANES, LANES)).T
            packed.append(by_lane_group([col[q * half:(q + 1) * half, :] for q in range(n_groups)]))
        ang = jnp.concatenate(packed, axis=0) * freq_ref[...]
        for tables, out_ref, scale in ((jnp.cos(ang), cos_ref, None), (jnp.sin(ang), sin_ref, sign_ref[...])):
            for c in range(ROPE_ROWS_PER_ITER):
                table = tables[c * half:(c + 1) * half, :]
                start = pl.multiple_of((it * ROPE_ROWS_PER_ITER + c) * LANES, LANES)
                shifted = [table] + [pltpu.roll(table, k * half, axis=1) for k in range(1, n_groups)]
                for q in range(n_groups):
                    full = by_lane_group([shifted[(g - q) % n_groups] for g in range(n_groups)])
                    if scale is not None:
                        full = full * scale
                    out_ref[0, pl.ds(start + q * half, half), :] = full
        return carry

    lax.fori_loop(0, n_rows // ROPE_ROWS_PER_ITER, body, 0)


def _rope_call(positions):
    bsz, seq = positions.shape
    half = HEAD_DIM // 2
    inv_freq = ROPE_BASE ** (-jnp.arange(half, dtype=F32) / half)
    lane = np.arange(LANES)
    freq = inv_freq[lane % half].reshape(1, LANES)
    sign = jnp.asarray(np.where((lane % HEAD_DIM) < half, -1.0, 1.0), F32).reshape(1, LANES)
    pos3 = positions.reshape(bsz, seq // LANES, LANES)
    out = jax.ShapeDtypeStruct((bsz, seq, LANES), F32)
    return pl.pallas_call(
        _rope_kernel,
        grid=(bsz,),
        in_specs=[pl.BlockSpec((1, seq // LANES, LANES), lambda b: (b, 0, 0)),
                  pl.BlockSpec((1, LANES), lambda b: (0, 0)),
                  pl.BlockSpec((1, LANES), lambda b: (0, 0))],
        out_specs=[pl.BlockSpec((1, seq, LANES), lambda b: (b, 0, 0)),
                   pl.BlockSpec((1, seq, LANES), lambda b: (b, 0, 0))],
        out_shape=[out, out],
        compiler_params=pltpu.CompilerParams(
            dimension_semantics=("arbitrary",), vmem_limit_bytes=VMEM_LIMIT),
        name="rope_table",
    )(pos3, freq, sign)


def _swap_halves(v):
    width = v.shape[1]
    fwd = pltpu.roll(v, HEAD_DIM // 2, axis=1)
    bwd = pltpu.roll(v, width - HEAD_DIM // 2, axis=1)
    first_half = _within(v.shape, 1, HEAD_DIM) < HEAD_DIM // 2
    return jnp.where(first_half, bwd, fwd)


def _causal_conv(ext_ref, row0, n_rows, w_ref, b_ref):
    x = ext_ref[pl.ds(row0 + SUBLANES, n_rows), :]
    before = ext_ref[pl.ds(row0, SUBLANES), :]
    sub = _iota(before.shape, 0)

    def delayed(v, v_before, d):
        rolled = pltpu.roll(v, d, axis=0)
        head = jnp.where(sub < d, pltpu.roll(v_before, d, axis=0), rolled[0:SUBLANES, :])
        return jnp.concatenate([head, rolled[SUBLANES:, :]], axis=0)

    w0, w1, w2, w3 = (w_ref[k:k + 1, :] for k in range(CONV_K))
    x_d = delayed(x, before, 1)
    older = w1 * x + w0 * x_d
    older_before = w1 * before + w0 * pltpu.roll(before, 1, axis=0)
    return b_ref[...] + (w3 * x + w2 * x_d) + delayed(older, older_before, 2)


GROUP_BASES = (OFF_Z, OFF_RQ, OFF_LG, OFF_GQ, PROJ_W)


class _GroupedColumns:
    def __init__(self, refs):
        self.refs = refs

    def _locate(self, cols):
        for g, ref in enumerate(self.refs):
            if GROUP_BASES[g] <= cols.start and cols.stop <= GROUP_BASES[g + 1]:
                return ref, slice(cols.start - GROUP_BASES[g], cols.stop - GROUP_BASES[g])
        raise ValueError(f"columns {cols} straddle head groups")

    def __getitem__(self, idx):
        ref, cols = self._locate(idx[1])
        return ref[idx[0], cols]

    def __setitem__(self, idx, value):
        ref, cols = self._locate(idx[1])
        ref[idx[0], cols] = value


def _mixer_kernel(x_ref, mod_ref, x_next_ref, mod_next_ref, cos_ref, sin_ref, w_in_ref, w_out_ref,
                  ssd_cw_ref, ssd_cb_ref, dtb_ref, alog_ref, dskip_ref, ssd_nw_ref, ret_nw_ref,
                  lru_cw_ref, lru_cb_ref, lru_w_ref, lru_b_ref, lam_ref,
                  wg2_ref, bg_ref, gla_nw_ref, ln_w_ref, ln_b_ref,
                  o_ref,
                  proj_ssd, proj_ret, proj_lru, proj_gla, y_ssd, y_ret, y_lru, y_gla,
                  xbc_ext, lx_ext, ssd_st, ret_st, gla_st, lru_st,
                  ret_l, ret_in, ret_end, ret_dec):
    proj = _GroupedColumns((proj_ssd, proj_ret, proj_lru, proj_gla))
    tc = x_ref.shape[1]
    n_chunks = tc // CHUNK
    step = pl.program_id(1)
    n_steps = pl.num_programs(1)
    body_rows = pl.ds(SUBLANES, tc)

    lane128 = _iota((CHUNK, LANES), 1)
    row128 = _iota((CHUNK, LANES), 0)
    low_half = lane128 < HEAD_DIM
    causal = row128 >= lane128
    tril = jnp.where(causal, 1.0, 0.0).astype(BF16)
    sq = (CHUNK, LANES)
    same_gla_chunk = _block_of(sq, 0, GLA_CHUNK) == _block_of(sq, 1, GLA_CHUNK)
    tril_gla = jnp.where(causal, jnp.where(same_gla_chunk, 1.0, 0.0), 0.0).astype(BF16)
    pair_diag = _block_of(sq, 0, HEAD_DIM) == _block_of(sq, 1, HEAD_DIM)

    head_of_lane = _block_of((1, GROUP_WIDTH), 1, HEAD_DIM)
    log_gamma = jnp.zeros((1, GROUP_WIDTH), F32)
    for h in range(N_HEADS):
        log_gamma = jnp.where(head_of_lane == h, math.log1p(-(2.0 ** (-5 - h))), log_gamma)

    def modulated(xr, mr):
        return (xr[0] * (1.0 + mr[0, 1:2, :]) + mr[0, 0:1, :]).astype(BF16)

    def project(h, first, last):
        value = _dot(h, w_in_ref[:, first:last])
        if OFF_XBC <= first and last <= OFF_DT:
            xbc_ext[body_rows, first - OFF_XBC:last - OFF_XBC] = value
        elif OFF_LX <= first and last <= OFF_GQ:
            lx_ext[body_rows, first - OFF_LX:last - OFF_LX] = value
        else:
            proj[:, first:last] = value

    def convolve():
        proj[:, OFF_XBC:OFF_XBC + SSD_XBC] = _silu(_causal_conv(xbc_ext, 0, tc, ssd_cw_ref, ssd_cb_ref))
        xbc_ext[0:SUBLANES, :] = xbc_ext[pl.ds(tc, SUBLANES), :]
        proj[:, OFF_LX:OFF_LX + GROUP_WIDTH] = _causal_conv(lx_ext, 0, tc, lru_cw_ref, lru_cb_ref)
        lx_ext[0:SUBLANES, :] = lx_ext[pl.ds(tc, SUBLANES), :]

    def clear_conv_tails():
        xbc_ext[0:SUBLANES, :] = jnp.zeros((SUBLANES, SSD_XBC), F32)
        lx_ext[0:SUBLANES, :] = jnp.zeros((SUBLANES, GROUP_WIDTH), F32)

    def pieces(h, first, last):
        return [functools.partial(project, h, lo, min(lo + GROUP_WIDTH, last))
                for lo in range(first, last, GROUP_WIDTH)]

    def all_pieces(h):
        return [p for g in range(len(GROUP_BASES) - 1) for p in pieces(h, GROUP_BASES[g], GROUP_BASES[g + 1])]

    @pl.when((pl.program_id(0) == 0) & (step == 0))
    def _first_block():
        clear_conv_tails()
        h_first = modulated(x_ref, mod_ref)
        for piece in all_pieces(h_first):
            piece()
        convolve()

    @pl.when(step == 0)
    def _init():
        ssd_st[...] = jnp.zeros_like(ssd_st)
        ret_st[...] = jnp.zeros_like(ret_st)
        gla_st[...] = jnp.zeros_like(gla_st)
        lru_st[...] = jnp.zeros_like(lru_st)
        t_col = _iota((CHUNK, GROUP_WIDTH), 0).astype(F32)
        ret_in[...] = jnp.exp((t_col + 1.0) * log_gamma)
        ret_end[...] = jnp.exp((CHUNK - 1.0 - t_col) * log_gamma)
        ret_dec[...] = jnp.exp(float(CHUNK) * jnp.broadcast_to(log_gamma, (SUBLANES, GROUP_WIDTH)))
        dist = (row128 - lane128).astype(F32)
        for h in range(N_HEADS):
            lg = math.log1p(-(2.0 ** (-5 - h)))
            ret_l[h // 2, :, (h % 2) * LANES:(h % 2 + 1) * LANES] = jnp.where(
                causal, jnp.exp(dist * lg), 0.0)

    gsq = (GROUP_WIDTH, GROUP_WIDTH)
    head_avg = jnp.where(_block_of(gsq, 0, HEAD_DIM) == _block_of(gsq, 1, HEAD_DIM),
                         1.0 / HEAD_DIM, 0.0).astype(BF16)
    gla_k_diag = _block_of((GROUP_WIDTH, LANES), 0, GLA_CHUNK) == _block_of((GROUP_WIDTH, LANES), 1, GLA_KEY_DIM)
    gla_v_diag = _block_of(gsq, 0, GLA_CHUNK) == _block_of(gsq, 1, HEAD_DIM)
    gla_causal = _iota((GLA_CHUNK, GROUP_WIDTH), 0) >= _within((GLA_CHUNK, GROUP_WIDTH), 1, GLA_CHUNK)
    gla_st_diag = (_block_of((LANES, GROUP_WIDTH), 0, GLA_KEY_DIM)
                   == _block_of((LANES, GROUP_WIDTH), 1, HEAD_DIM))
    gla_row_chunk = _block_of((CHUNK, GROUP_WIDTH), 0, GLA_CHUNK)

    def ssd_task(c):
        rows = pl.ds(c * CHUNK, CHUNK)
        dt = _softplus(proj[rows, OFF_DT:OFF_DT + GROUP_WIDTH] + dtb_ref[...])
        log_a = -jnp.exp(alog_ref[...]) * dt
        cum = _split_dot(tril, log_a)
        xs = proj[rows, OFF_XBC:OFF_XBC + GROUP_WIDTH]
        z = proj[rows, OFF_Z:OFF_Z + GROUP_WIDTH]
        xdt = xs * dt
        b_ts, c_gs, scores = [], [], []
        for g in range(2):
            b_g = proj[rows, OFF_XBC + GROUP_WIDTH + g * LANES:OFF_XBC + GROUP_WIDTH + (g + 1) * LANES]
            c_g = proj[rows, OFF_XBC + 2 * GROUP_WIDTH + g * LANES:
                       OFF_XBC + 2 * GROUP_WIDTH + (g + 1) * LANES].astype(BF16)
            b_t = b_g.T.astype(BF16)
            b_ts.append(b_t)
            c_gs.append(c_g)
            scores.append(_dot(c_g, b_t))
        yield
        cum_last = cum[CHUNK - 1:CHUNK, :]
        to_end = jnp.exp(cum_last - cum)
        from_start = jnp.exp(cum)
        y_intra, new_states = [], []
        for g in range(2):
            gl = slice(g * LANES, (g + 1) * LANES)
            cum_g = cum[:, gl]
            cum_sw = pltpu.roll(cum_g, HEAD_DIM, axis=1)
            col_a = jnp.where(low_half, cum_g, cum_sw)
            col_b = jnp.where(low_half, cum_sw, cum_g)
            cum_t = cum_g.T
            l_a = jnp.exp(jnp.where(causal, col_a - cum_t[0:1, :], -jnp.inf))
            l_b = jnp.exp(jnp.where(causal, col_b - cum_t[HEAD_DIM:HEAD_DIM + 1, :], -jnp.inf))
            p = jnp.concatenate([scores[g] * l_a, scores[g] * l_b], axis=1).astype(BF16)
            xdt_g = xdt[:, gl]
            v_bd = jnp.concatenate([jnp.where(low_half, xdt_g, 0.0),
                                    jnp.where(low_half, 0.0, xdt_g)], axis=0).astype(BF16)
            y_intra.append(_dot(p, v_bd))
            new_states.append(_dot(b_ts[g], (xdt_g * to_end[:, gl]).astype(BF16)))
        yield
        y_inter = []
        for g in range(2):
            gl = slice(g * LANES, (g + 1) * LANES)
            prev = ssd_st[g]
            y_inter.append(_dot(c_gs[g], prev.astype(BF16)))
            ssd_st[g] = prev * jnp.exp(cum_last[:, gl]) + new_states[g]
        yield
        y = jnp.concatenate([y_intra[g] + from_start[:, g * LANES:(g + 1) * LANES] * y_inter[g]
                             for g in range(2)], axis=1) + xs * dskip_ref[...]
        y = y * _silu(z)
        y = y * lax.rsqrt(jnp.mean(y * y, axis=-1, keepdims=True) + EPS) * ssd_nw_ref[...]
        y_ssd[rows, :] = y.astype(BF16)

    def ret_task(c):
        rows = pl.ds(c * CHUNK, CHUNK)
        cos = cos_ref[0, rows, :]
        sin = sin_ref[0, rows, :]
        cos2 = jnp.concatenate([cos, cos], axis=1)
        sin2 = jnp.concatenate([sin, sin], axis=1)
        q = proj[rows, OFF_RQ:OFF_RQ + GROUP_WIDTH]
        k = proj[rows, OFF_RK:OFF_RK + GROUP_WIDTH]
        q = q * cos2 + _swap_halves(q) * sin2
        k = (k * cos2 + _swap_halves(k) * sin2) * (HEAD_DIM ** -0.5)
        v = proj[rows, OFF_RV:OFF_RV + GROUP_WIDTH]
        out_gate = proj[rows, OFF_RG:OFF_RG + GROUP_WIDTH]
        v_end = v * ret_end[...]
        q_bf = q.astype(BF16)
        scores, new_states = [], []
        for p_i in range(2):
            pl_ = slice(p_i * LANES, (p_i + 1) * LANES)
            k_t = k[:, pl_].T
            k_bd = jnp.concatenate([jnp.where(row128 < HEAD_DIM, k_t, 0.0),
                                    jnp.where(row128 < HEAD_DIM, 0.0, k_t)], axis=1).astype(BF16)
            scores.append(_dot(q_bf[:, pl_], k_bd))
            new_states.append(_dot(k_t.astype(BF16), v_end[:, pl_].astype(BF16)))
        yield
        y_intra, y_inter = [], []
        for p_i in range(2):
            pl_ = slice(p_i * LANES, (p_i + 1) * LANES)
            p = (scores[p_i] * ret_l[p_i]).astype(BF16)
            v_p = v[:, pl_]
            v_bd = jnp.concatenate([jnp.where(low_half, v_p, 0.0),
                                    jnp.where(low_half, 0.0, v_p)], axis=0).astype(BF16)
            y_intra.append(_dot(p, v_bd))
            prev = ret_st[p_i]
            y_inter.append(_dot(q_bf[:, pl_], prev.astype(BF16)))
            ret_st[p_i] = prev * ret_dec[0:1, pl_] + jnp.where(pair_diag, new_states[p_i], 0.0)
        yield
        y = jnp.concatenate([y_intra[p_i] + ret_in[:, p_i * LANES:(p_i + 1) * LANES] * y_inter[p_i]
                             for p_i in range(2)], axis=1)
        mu = _dot(y.astype(BF16), head_avg)
        yield
        yc = y - mu
        var = _dot((yc * yc).astype(BF16), head_avg)
        yield
        y = yc * lax.rsqrt(var + EPS) * ret_nw_ref[...]
        y_ret[rows, :] = (y * _silu(out_gate)).astype(BF16)

    def lru_task(c):
        rows = pl.ds(c * CHUNK, CHUNK)
        xr = proj[rows, OFF_LX:OFF_LX + GROUP_WIDTH]
        gates = _dot(xr.astype(BF16), lru_w_ref[...])
        out_gate = proj[rows, OFF_LG:OFF_LG + GROUP_WIDTH]
        yield
        gates = gates + lru_b_ref[...]
        r_gate = _sigmoid(gates[:, 0:GROUP_WIDTH])
        i_gate = _sigmoid(gates[:, GROUP_WIDTH:2 * GROUP_WIDTH])
        log_a = -LRU_C * r_gate * _softplus(-lam_ref[...])
        a = jnp.exp(log_a)
        u = _sqrt_nonneg(1.0 - jnp.exp(2.0 * log_a)) * (i_gate * xr)
        n_groups = CHUNK // SUBLANES
        a3 = a.reshape(n_groups, SUBLANES, GROUP_WIDTH)
        u3 = u.reshape(n_groups, SUBLANES, GROUP_WIDTH)
        sub = _iota((1, SUBLANES, GROUP_WIDTH), 1)
        d = 1
        while d < SUBLANES:
            keep = sub >= d
            u3 = jnp.where(keep, a3 * pltpu.roll(u3, d, axis=1) + u3, u3)
            a3 = jnp.where(keep, a3 * pltpu.roll(a3, d, axis=1), a3)
            d *= 2
        h_prev = lru_st[0:1, :]
        gelu_gate = _gelu_tanh(out_gate)
        for grp in range(n_groups):
            h_grp = u3[grp] + a3[grp] * h_prev
            h_prev = h_grp[SUBLANES - 1:SUBLANES, :]
            r0 = c * CHUNK + grp * SUBLANES
            y_lru[r0:r0 + SUBLANES, :] = (
                h_grp * gelu_gate[grp * SUBLANES:(grp + 1) * SUBLANES, :]).astype(BF16)
        lru_st[...] = jnp.broadcast_to(h_prev, lru_st.shape)

    def gla_task(c):
        rows = pl.ds(c * CHUNK, CHUNK)
        g_low = proj[rows, OFF_GLOW:OFF_GLOW + LANES].astype(BF16)
        gate_pre = _dot(g_low, wg2_ref[...])
        q = proj[rows, OFF_GQ:OFF_GQ + LANES] * (GLA_KEY_DIM ** -0.5)
        k = proj[rows, OFF_GK:OFF_GK + LANES]
        v = proj[rows, OFF_GV:OFF_GV + GROUP_WIDTH]
        out_gate = proj[rows, OFF_GR:OFF_GR + GROUP_WIDTH]
        yield
        log_a = -_softplus(-(gate_pre + bg_ref[...])) * (1.0 / GLA_GATE_NORM)
        cum = _split_dot(tril_gla, log_a)
        yield
        first = row128 < GLA_CHUNK
        cum_end = jnp.where(first, cum[GLA_CHUNK - 1:GLA_CHUNK, :], cum[CHUNK - 1:CHUNK, :])
        q_in = (q * jnp.exp(cum)).astype(BF16)
        k_in = k * jnp.exp(-cum)
        k_end_t = (k * jnp.exp(cum_end - cum)).T.astype(BF16)
        dec_t = jnp.exp(cum_end).T
        dec_sw = pltpu.roll(dec_t, GLA_CHUNK, axis=1)
        dec_by_chunk = (jnp.where(low_half, dec_t, dec_sw), jnp.where(low_half, dec_sw, dec_t))
        n_sub = CHUNK // GLA_CHUNK
        scores, new_states = [], []
        for cc in range(n_sub):
            rr = slice(cc * GLA_CHUNK, (cc + 1) * GLA_CHUNK)
            k_rep = jnp.concatenate([k_in[rr, :]] * N_HEADS, axis=0)
            k_bd = jnp.where(gla_k_diag, k_rep, 0.0).astype(BF16)
            scores.append(_dot_nt(q_in[rr, :], k_bd))
            v_only = jnp.where(gla_row_chunk == cc, v, 0.0).astype(BF16)
            new_states.append(_dot(k_end_t, v_only))
        yield
        y_intra, y_inter = [], []
        for cc in range(n_sub):
            rr = slice(cc * GLA_CHUNK, (cc + 1) * GLA_CHUNK)
            masked = jnp.where(gla_causal, scores[cc], 0.0).astype(BF16)
            v_rep = jnp.concatenate([v[rr, :]] * N_HEADS, axis=0)
            v_bd = jnp.where(gla_v_diag, v_rep, 0.0).astype(BF16)
            y_intra.append(_dot(masked, v_bd))
            prev = gla_st[...]
            y_inter.append(_dot(q_in[rr, :], prev.astype(BF16)))
            dec = jnp.concatenate([dec_by_chunk[cc]] * 2, axis=1)
            gla_st[...] = prev * dec + jnp.where(gla_st_diag, new_states[cc], 0.0)
        yield
        y = jnp.concatenate([y_intra[cc] + y_inter[cc] for cc in range(n_sub)], axis=0)
        ms = _dot((y * y).astype(BF16), head_avg)
        yield
        y = y * lax.rsqrt(ms + EPS) * gla_nw_ref[...]
        y_gla[rows, :] = (y * _silu(out_gate)).astype(BF16)

    h_next = modulated(x_next_ref, mod_next_ref)
    tasks = [task(c) for task in (ssd_task, ret_task, lru_task, gla_task) for c in range(n_chunks)]
    for piece in pieces(h_next, OFF_XBC, OFF_DT) + pieces(h_next, OFF_LX, OFF_GQ):
        piece()
    fillers = (pieces(h_next, OFF_Z, OFF_XBC) + pieces(h_next, OFF_DT, OFF_LG)
               + pieces(h_next, OFF_LG, OFF_LX) + pieces(h_next, OFF_GQ, PROJ_W))
    n_rounds = 6
    per_round = -(-len(fillers) // (n_rounds - 1))
    while tasks:
        alive = []
        for task in tasks:
            try:
                next(task)
                alive.append(task)
            except StopIteration:
                pass
        tasks = alive
        for filler in fillers[:per_round]:
            filler()
        fillers = fillers[per_round:]
    for filler in fillers:
        filler()

    x = x_ref[0]
    gate = mod_ref[0, 2:3, :]
    y_all = jnp.concatenate([y_ssd[...], y_ret[...], y_lru[...], y_gla[...]], axis=1)
    mixed = _dot(y_all, w_out_ref[...])

    keep_tail = jnp.where(step == n_steps - 1, 0.0, 1.0)
    xbc_ext[0:SUBLANES, :] = xbc_ext[0:SUBLANES, :] * keep_tail
    lx_ext[0:SUBLANES, :] = lx_ext[0:SUBLANES, :] * keep_tail
    convolve()
    o_ref[0] = _layer_norm(DEEPNORM_ALPHA * x + gate * mixed, ln_w_ref[...], ln_b_ref[...])


class _OfLayer(NamedTuple):
    stacked: jax.Array
    layer: int


def _array_of(param):
    return param.stacked if isinstance(param, _OfLayer) else param


def _resident_spec(param):
    if isinstance(param, _OfLayer):
        shape = param.stacked.shape
        return pl.BlockSpec((None,) + shape[1:],
                            lambda *_, _l=param.layer, _n=len(shape): (_l,) + (0,) * (_n - 1))
    return pl.BlockSpec(param.shape, lambda *_, _n=param.ndim: (0,) * _n)


def _mixer_call(x, mod, cos, sin, params):
    bsz, seq, d = x.shape
    tc = MIX_TOKENS
    n_steps = seq // tc
    tok_spec = lambda w: pl.BlockSpec((1, tc, w), lambda b, s: (b, s, 0))

    def next_batch(b, s):
        return jnp.minimum(b + (s + 1) // n_steps, bsz - 1)

    in_specs = [tok_spec(d),
                pl.BlockSpec((1, 3, d), lambda b, s: (b, 0, 0)),
                pl.BlockSpec((1, tc, d), lambda b, s: (next_batch(b, s), (s + 1) % n_steps, 0)),
                pl.BlockSpec((1, 3, d), lambda b, s: (next_batch(b, s), 0, 0)),
                tok_spec(LANES), tok_spec(LANES)] + [_resident_spec(p) for p in params]
    scratch = [
        pltpu.VMEM((tc, OFF_RQ - OFF_Z), F32),
        pltpu.VMEM((tc, OFF_LG - OFF_RQ), F32),
        pltpu.VMEM((tc, OFF_GQ - OFF_LG), F32),
        pltpu.VMEM((tc, PROJ_W - OFF_GQ), F32),
        pltpu.VMEM((tc, GROUP_WIDTH), BF16),
        pltpu.VMEM((tc, GROUP_WIDTH), BF16),
        pltpu.VMEM((tc, GROUP_WIDTH), BF16),
        pltpu.VMEM((tc, GROUP_WIDTH), BF16),
        pltpu.VMEM((tc + SUBLANES, SSD_XBC), F32),
        pltpu.VMEM((tc + SUBLANES, GROUP_WIDTH), F32),
        pltpu.VMEM((2, SSD_STATE, LANES), F32),
        pltpu.VMEM((2, LANES, LANES), F32),
        pltpu.VMEM((LANES, GROUP_WIDTH), F32),
        pltpu.VMEM((SUBLANES, GROUP_WIDTH), F32),
        pltpu.VMEM((2, CHUNK, GROUP_WIDTH), F32),
        pltpu.VMEM((CHUNK, GROUP_WIDTH), F32),
        pltpu.VMEM((CHUNK, GROUP_WIDTH), F32),
        pltpu.VMEM((SUBLANES, GROUP_WIDTH), F32),
    ]
    return pl.pallas_call(
        _mixer_kernel,
        grid=(bsz, seq // tc),
        in_specs=in_specs,
        out_specs=pl.BlockSpec((1, tc, d), lambda b, s: (b, s, 0)),
        out_shape=jax.ShapeDtypeStruct((bsz, seq, d), F32),
        scratch_shapes=scratch,
        compiler_params=pltpu.CompilerParams(
            dimension_semantics=("arbitrary", "arbitrary"), vmem_limit_bytes=VMEM_LIMIT),
        name="token_mixer",
    )(x, mod, x, mod, cos, sin, *[_array_of(p) for p in params])


def _ffn_kernel(x_ref, mod_ref, x_prev_ref, mod_prev_ref, w_up_ref, w_down_ref, ln_w_ref, ln_b_ref,
                o_ref, acc_ref, pre_ln_ref):
    step = pl.program_id(0)
    tm = x_ref.shape[1]
    n_inner = D_FF // FFN_BLOCK
    strip = tm // FFN_LN_STRIPS

    @pl.when(step == 0)
    def _nothing_to_normalise_yet():
        pre_ln_ref[...] = jnp.zeros_like(pre_ln_ref)

    runtime_zero = (step >> 30).astype(F32)
    gate_prev = mod_prev_ref[0, 2:3, :]
    h_in = (x_ref[0] * (1.0 + mod_ref[0, 1:2, :]) + mod_ref[0, 0:1, :]).astype(BF16)
    for j in range(n_inner):
        cols = slice(j * FFN_BLOCK, (j + 1) * FFN_BLOCK)
        g = _dot(h_in, w_up_ref[:, cols])
        u = _dot(h_in, w_up_ref[:, D_FF + j * FFN_BLOCK:D_FF + (j + 1) * FFN_BLOCK])
        if j < FFN_LN_STRIPS:
            rows = pl.ds(j * strip, strip)
            normed = _layer_norm(DEEPNORM_ALPHA * x_prev_ref[0, rows, :] + gate_prev * pre_ln_ref[rows, :],
                                 ln_w_ref[...], ln_b_ref[...])
            o_ref[0, rows, :] = normed
            folded = normed.reshape(strip // SUBLANES, SUBLANES, normed.shape[1]).sum(axis=0)
            tie = runtime_zero * sum(folded[:, k * FFN_BLOCK:(k + 1) * FFN_BLOCK]
                                     for k in range(folded.shape[1] // FFN_BLOCK))
            u = jnp.concatenate([u[0:SUBLANES, :] + tie, u[SUBLANES:, :]], axis=0)
        act = (_silu(g) * u).astype(BF16)
        part = _dot(act, w_down_ref[cols, :])
        if j == 0:
            acc_ref[...] = part
        elif j < n_inner - 1:
            acc_ref[...] += part
        else:
            pre_ln_ref[...] = acc_ref[...] + part


def _ffn_call(x, mod, w_up, w_down, ln_w, ln_b):
    bsz, seq, d = x.shape
    tm = FFN_TOKENS
    per_seq = seq // tm
    n_blocks = bsz * per_seq

    def cur(i):
        blk = jnp.minimum(i, n_blocks - 1)
        return blk // per_seq, blk % per_seq

    def prev(i):
        blk = jnp.maximum(i - 1, 0)
        return blk // per_seq, blk % per_seq

    return pl.pallas_call(
        _ffn_kernel,
        grid=(n_blocks + 1,),
        in_specs=[pl.BlockSpec((1, tm, d), lambda i: (*cur(i), 0)),
                  pl.BlockSpec((1, 3, d), lambda i: (cur(i)[0], 0, 0)),
                  pl.BlockSpec((1, tm, d), lambda i: (*prev(i), 0)),
                  pl.BlockSpec((1, 3, d), lambda i: (prev(i)[0], 0, 0)),
                  _resident_spec(w_up), _resident_spec(w_down), _resident_spec(ln_w), _resident_spec(ln_b)],
        out_specs=pl.BlockSpec((1, tm, d), lambda i: (*prev(i), 0)),
        out_shape=jax.ShapeDtypeStruct((bsz, seq, d), F32),
        scratch_shapes=[pltpu.VMEM((tm, d), F32), pltpu.VMEM((tm, d), F32)],
        compiler_params=pltpu.CompilerParams(
            dimension_semantics=("arbitrary",), vmem_limit_bytes=VMEM_LIMIT),
        name="swiglu_ffn",
    )(x, mod, x, mod, _array_of(w_up), _array_of(w_down), ln_w, ln_b)


def _relayout_w_in(w_in):
    starts = np.concatenate([[0], np.cumsum(IN_SPLITS)]).tolist()
    (z, xbc, dt, rq, rk, rv, rg, lg, lx, gq, gk, gv, glow, gr) = [
        w_in[..., starts[i]:starts[i + 1]] for i in range(len(IN_SPLITS))]
    dt_rep = jnp.repeat(dt, HEAD_DIM, axis=-1)
    glow_pad = jnp.pad(glow, ((0, 0), (0, 0), (0, LANES - GLA_GATE_RANK)))
    out = jnp.concatenate([z, xbc, dt_rep, rq, rk, rv, rg, lg, lx, gq, gk, gv, gr, glow_pad], axis=-1)
    assert out.shape[-1] == PROJ_W
    return out.astype(BF16)


def _block_diag4(w):
    out = jnp.zeros((GROUP_WIDTH, GROUP_WIDTH), w.dtype)
    for i in range(w.shape[0]):
        out = out.at[i * HEAD_DIM:(i + 1) * HEAD_DIM, i * HEAD_DIM:(i + 1) * HEAD_DIM].set(w[i])
    return out


def _row(v):
    return v.reshape(1, -1).astype(F32)


def _rep_heads(v):
    return jnp.repeat(v.astype(F32), HEAD_DIM).reshape(1, GROUP_WIDTH)


def kernel(x, c, positions, w_ada, b_ada, w_in, ssd_conv_w, ssd_conv_b, ssd_dt_bias, ssd_a_log, ssd_d,
           ssd_norm_w, ret_norm_w, lru_conv_w, lru_conv_b, lru_wa, lru_ba, lru_wx, lru_bx, lru_lambda,
           gla_wg2, gla_bg, gla_norm_w, w_out, ln1_w, ln1_b, ffn_w_up, ffn_w_down, ln2_w, ln2_b):
    bsz = x.shape[0]
    depth = w_in.shape[0]
    mod = _ada_call(c, w_ada, b_ada).reshape(depth, bsz, 6, D_MODEL)
    cos, sin = _rope_call(positions)

    w_in_all = _relayout_w_in(w_in)
    w_out_all = w_out.astype(BF16)
    w_up_all = ffn_w_up.astype(BF16)
    w_down_all = ffn_w_down.astype(BF16)

    for l in range(depth):
        wg2_p = jnp.zeros((LANES, LANES), F32).at[0:GLA_GATE_RANK, :].set(gla_wg2[l]).astype(BF16)
        lru_w = jnp.concatenate([_block_diag4(lru_wa[l]), _block_diag4(lru_wx[l])], axis=1).astype(BF16)
        params = (
            _OfLayer(w_in_all, l), _OfLayer(w_out_all, l),
            ssd_conv_w[l], _row(ssd_conv_b[l]), _rep_heads(ssd_dt_bias[l]), _rep_heads(ssd_a_log[l]),
            _rep_heads(ssd_d[l]), _row(ssd_norm_w[l]), _row(ret_norm_w[l]),
            lru_conv_w[l], _row(lru_conv_b[l]), lru_w,
            _row(jnp.concatenate([lru_ba[l], lru_bx[l]])), _row(lru_lambda[l]),
            wg2_p, _row(gla_bg[l]), _row(gla_norm_w[l]), _row(ln1_w[l]), _row(ln1_b[l]),
        )
        x = _mixer_call(x, mod[l, :, 0:3, :], cos, sin, params)
        x = _ffn_call(x, mod[l, :, 3:6, :], _OfLayer(w_up_all, l), _OfLayer(w_down_all, l),
                      _row(ln2_w[l]), _row(ln2_b[l]))
    return x
```
